```python
import math
import jax
import jax.numpy as jnp
from jax import lax
import numpy as np

D_MODEL = 2048
BATCH = 32
SEQ = 256
DEPTH = 4
DEC_BATCH = 2
DEC_SEQ = 4096
PAST_LEN = 256

GRID_W = 64
N_MIXERS = 3
N_ATTN = (DEPTH + 2) // 3
N_POOL = (DEPTH + 1) // 3
N_CHUNK = DEPTH // 3
EPS = 1e-6

N_HEADS = 8
HEAD_DIM = 128
V_DIM = 2 * HEAD_DIM
ROPE_THETA = 10000.0
Q_BLOCK = 128

POOL_WINDOWS = (2, 4, 8, 16)
POOL_GROUP = D_MODEL // 4

CHUNK = 128
GMLP_WIDTH = 2 * D_MODEL
GMLP_GROUPS = 8

N_EXPERTS = 32
TOP_K = 4
D_EXPERT = D_MODEL
SWIGLU_LIMIT = 7.0
SWIGLU_ALPHA = 1.702
MOE_BLOCK = 128

kernel_name = "hybrid_diffattn_pool_gmlp_moe_dit_step"

F32 = jnp.float32


def rms_norm(x, g):
    xf = x.astype(F32)
    y = xf * lax.rsqrt(jnp.mean(xf * xf, axis=-1, keepdims=True) + EPS)
    return (y * g.astype(F32)).astype(x.dtype)


def layer_norm(x, g, b):
    xf = x.astype(F32)
    mu = jnp.mean(xf, axis=-1, keepdims=True)
    xc = xf - mu
    y = xc * lax.rsqrt(jnp.mean(xc * xc, axis=-1, keepdims=True) + EPS)
    return (y * g.astype(F32) + b.astype(F32)).astype(x.dtype)


def modulation(cond, w_mod, b_mod):
    m = jax.nn.silu(cond) @ w_mod + b_mod
    return [t[:, None, :] for t in jnp.split(m, 6, axis=-1)]


def modulate(h, shift, scale):
    return h * (1 + scale) + shift


def axial_rope(n_tokens):
    rows = n_tokens // GRID_W
    t = jnp.arange(rows * GRID_W)
    row = (t // GRID_W).astype(F32)
    col = (t % GRID_W).astype(F32)
    n_freq = HEAD_DIM // 4
    inv = ROPE_THETA ** (-jnp.arange(n_freq, dtype=F32) / n_freq)
    ang = jnp.stack([row[:, None] * inv, col[:, None] * inv], axis=1)
    ang = jnp.stack([ang, ang], axis=2)
    return jnp.cos(ang), jnp.sin(ang)


def apply_rope(x, cos, sin):
    shp = x.shape
    xr = x.reshape(shp[:-1] + (2, 2, HEAD_DIM // 4))
    rot = jnp.stack([-xr[..., 1, :], xr[..., 0, :]], axis=-2)
    c = cos[:, None, None].astype(x.dtype)
    s = sin[:, None, None].astype(x.dtype)
    return (xr * c + rot * s).reshape(shp)


def attn_qkv(h, w_qkv, q_g, k_g):
    b, l, _ = h.shape
    p = h @ w_qkv
    nq = 2 * N_HEADS * HEAD_DIM
    q = rms_norm(p[..., :nq].reshape(b, l, N_HEADS, 2, HEAD_DIM), q_g)
    k = rms_norm(p[..., nq:2 * nq].reshape(b, l, N_HEADS, 2, HEAD_DIM), k_g)
    v = p[..., 2 * nq:].reshape(b, l, N_HEADS, V_DIM)
    return q, k, v


def diff_attn_core(q, k, v, lam, subln_g, lam_init):
    b, nq = q.shape[:2]
    s = jnp.einsum("bqhmd,bkhmd->bhmqk", q, k, preferred_element_type=F32) * (HEAD_DIM ** -0.5)
    p = jax.nn.softmax(s, axis=-1)
    a = p[:, :, 0] - lam * p[:, :, 1]
    o = jnp.einsum("bhqk,bkhe->bqhe", a.astype(v.dtype), v)
    o = rms_norm(o, subln_g) * (1.0 - lam_init)
    return o.reshape(b, nq, N_HEADS * V_DIM)


def blocked_diff_attn(q, k, v, lam, subln_g, lam_init):
    b, s = q.shape[:2]
    nb = s // Q_BLOCK
    qb = q.reshape(b, nb, Q_BLOCK, N_HEADS, 2, HEAD_DIM).swapaxes(0, 1)
    ob = lax.map(lambda qi: diff_attn_core(qi, k, v, lam, subln_g, lam_init), qb)
    return ob.swapaxes(0, 1).reshape(b, s, N_HEADS * V_DIM)


def pool_mixer(h, w_pool, scale):
    b, l, d = h.shape
    hf = h.astype(F32)
    cs = jnp.concatenate([jnp.zeros((b, 1, d), F32), jnp.cumsum(hf, axis=1)], axis=1)
    t = jnp.arange(l)
    outs = []
    for g, w in enumerate(POOL_WINDOWS):
        lo = jnp.clip(t - w // 2, 0, l)
        hi = jnp.clip(t + w // 2, 0, l)
        sl = slice(g * POOL_GROUP, (g + 1) * POOL_GROUP)
        csg = cs[..., sl]
        mean = (csg[:, hi] - csg[:, lo]) / (hi - lo).astype(F32)[None, :, None]
        diff = (mean - hf[..., sl]).astype(h.dtype)
        outs.append(diff @ w_pool[g])
    return jnp.concatenate(outs, axis=-1) * scale


def gmlp_mixer(h, w_in, ln_g, ln_b, w_s, b_s, w_out):
    b, l, _ = h.shape
    z = jax.nn.gelu(h @ w_in)
    u, v = z[..., :GMLP_WIDTH], z[..., GMLP_WIDTH:]
    v = layer_norm(v, ln_g, ln_b)
    v = v.reshape(b, l // CHUNK, CHUNK, GMLP_GROUPS, GMLP_WIDTH // GMLP_GROUPS)
    v = jnp.einsum("gpq,bnqgc->bnpgc", w_s, v) + b_s.T[:, :, None]
    return (u * v.reshape(b, l, GMLP_WIDTH)) @ w_out


def moe(h, w_router, b_router, w_gu, b_gu, w_down, b_down):
    b, l, d = h.shape
    x = h.reshape(-1, d)
    n_tok = x.shape[0]
    logits = (x @ w_router + b_router).astype(F32)
    top_val, top_idx = lax.top_k(logits, TOP_K)
    gates = jax.nn.softmax(top_val, axis=-1).astype(x.dtype)
    n_assign = n_tok * TOP_K
    flat_e = top_idx.reshape(-1)
    order = jnp.argsort(flat_e)
    e_sorted = flat_e[order]
    tok_sorted = order // TOP_K
    gate_sorted = gates.reshape(-1)[order]
    counts = jnp.bincount(flat_e, length=N_EXPERTS)
    padded = (counts + MOE_BLOCK - 1) // MOE_BLOCK * MOE_BLOCK
    pend = jnp.cumsum(padded)
    pstart = pend - padded
    start = jnp.cumsum(counts) - counts
    dest = pstart[e_sorted] + jnp.arange(n_assign) - start[e_sorted]
    n_blocks = -(-n_assign // MOE_BLOCK) + N_EXPERTS
    rows_tok = jnp.full((n_blocks * MOE_BLOCK,), n_tok, jnp.int32).at[dest].set(tok_sorted)
    block_e = jnp.minimum(jnp.searchsorted(pend, jnp.arange(n_blocks) * MOE_BLOCK, side="right"), N_EXPERTS - 1)
    x_pad = jnp.concatenate([x, jnp.zeros((1, d), x.dtype)], axis=0)
    xb = x_pad[rows_tok].reshape(n_blocks, MOE_BLOCK, d)

    def expert_block(args):
        xi, e = args
        gu = xi @ w_gu[e] + b_gu[e]
        gate = jnp.minimum(gu[:, :D_EXPERT], SWIGLU_LIMIT)
        up = jnp.clip(gu[:, D_EXPERT:], -SWIGLU_LIMIT, SWIGLU_LIMIT)
        act = gate * jax.nn.sigmoid(SWIGLU_ALPHA * gate) * (up + 1)
        return act @ w_down[e] + b_down[e]

    yb = lax.map(expert_block, (xb, block_e)).reshape(-1, d)
    contrib = yb[dest] * gate_sorted[:, None]
    out = jnp.zeros((n_tok, d), x.dtype).at[tok_sorted].add(contrib)
    return out.reshape(b, l, d)


def setup_inputs(seed: int = 0) -> dict:
    key = jax.random.key(seed)
    keys = iter(jax.random.split(key, 40))

    def nrm(shape, s):
        return jax.random.normal(next(keys), shape, F32) * s

    D = D_MODEL
    inp = {}
    inp["x_prompt"] = nrm((BATCH, SEQ, D), 1.0)
    inp["x_sample"] = nrm((DEC_BATCH, DEC_SEQ, D), 1.0)
    inp["c"] = nrm((DEC_BATCH, D), 1.0)
    inp["cache_k"] = nrm((DEC_BATCH, N_ATTN, PAST_LEN, N_HEADS, 2, HEAD_DIM), 1.0)
    inp["cache_v"] = nrm((DEC_BATCH, N_ATTN, PAST_LEN, N_HEADS, V_DIM), 1.0)
    inp["c_ctx"] = nrm((D,), 1.0)
    inp["w_mod"] = nrm((DEPTH, D, 6 * D), 0.5 * D ** -0.5)
    inp["b_mod"] = nrm((DEPTH, 6 * D), 0.02)
    inp["norm1_g"] = 1.0 + nrm((DEPTH, D), 0.02)
    inp["norm2_g"] = 1.0 + nrm((DEPTH, D), 0.02)
    inp["attn_w_qkv"] = nrm((N_ATTN, D, 4 * N_HEADS * HEAD_DIM + N_HEADS * V_DIM), D ** -0.5)
    inp["attn_q_norm"] = 1.0 + nrm((N_ATTN, HEAD_DIM), 0.02)
    inp["attn_k_norm"] = 1.0 + nrm((N_ATTN, HEAD_DIM), 0.02)
    inp["attn_lambda_q1"] = nrm((N_ATTN, HEAD_DIM), 0.1)
    inp["attn_lambda_k1"] = nrm((N_ATTN, HEAD_DIM), 0.1)
    inp["attn_lambda_q2"] = nrm((N_ATTN, HEAD_DIM), 0.1)
    inp["attn_lambda_k2"] = nrm((N_ATTN, HEAD_DIM), 0.1)
    inp["attn_subln_g"] = 1.0 + nrm((N_ATTN, V_DIM), 0.02)
    inp["attn_w_o"] = nrm((N_ATTN, N_HEADS * V_DIM, D), (N_HEADS * V_DIM) ** -0.5)
    inp["pool_w"] = nrm((N_POOL, 4, POOL_GROUP, POOL_GROUP), POOL_GROUP ** -0.5)
    inp["pool_scale"] = 1.0 + nrm((N_POOL, D), 0.02)
    inp["gmlp_w_in"] = nrm((N_CHUNK, D, 2 * GMLP_WIDTH), D ** -0.5)
    inp["gmlp_ln_g"] = 1.0 + nrm((N_CHUNK, GMLP_WIDTH), 0.02)
    inp["gmlp_ln_b"] = nrm((N_CHUNK, GMLP_WIDTH), 0.02)
    inp["gmlp_w_s"] = nrm((N_CHUNK, GMLP_GROUPS, CHUNK, CHUNK), CHUNK ** -0.5)
    inp["gmlp_b_s"] = 1.0 + nrm((N_CHUNK, GMLP_GROUPS, CHUNK), 0.02)
    inp["gmlp_w_out"] = nrm((N_CHUNK, GMLP_WIDTH, D), GMLP_WIDTH ** -0.5)
    inp["moe_w_router"] = nrm((DEPTH, D, N_EXPERTS), D ** -0.5)
    inp["moe_b_router"] = nrm((DEPTH, N_EXPERTS), 0.01)
    inp["moe_w_gu"] = nrm((DEPTH, N_EXPERTS, D, 2 * D_EXPERT), D ** -0.5)
    inp["moe_b_gu"] = nrm((DEPTH, N_EXPERTS, 2 * D_EXPERT), 0.02)
    inp["moe_w_down"] = nrm((DEPTH, N_EXPERTS, D_EXPERT, D), D_EXPERT ** -0.5)
    inp["moe_b_down"] = nrm((DEPTH, N_EXPERTS, D), 0.02)
    return inp


def reference(x_prompt, x_sample, c, cache_k, cache_v, c_ctx, w_mod, b_mod, norm1_g, norm2_g,
              attn_w_qkv, attn_q_norm, attn_k_norm, attn_lambda_q1, attn_lambda_k1,
              attn_lambda_q2, attn_lambda_k2, attn_subln_g, attn_w_o,
              pool_w, pool_scale,
              gmlp_w_in, gmlp_ln_g, gmlp_ln_b, gmlp_w_s, gmlp_b_s, gmlp_w_out,
              moe_w_router, moe_b_router, moe_w_gu, moe_b_gu, moe_w_down, moe_b_down):
    xc = x_prompt
    xl = x_sample
    cos, sin = axial_rope(x_sample.shape[1])
    ctx_k, ctx_v = [], []
    for layer in range(DEPTH):
        kind, j = layer % N_MIXERS, layer // N_MIXERS
        mc = modulation(c_ctx[None, :], w_mod[layer], b_mod[layer])
        ml = modulation(c, w_mod[layer], b_mod[layer])
        hc = modulate(rms_norm(xc, norm1_g[layer]), mc[0], mc[1])
        hl = modulate(rms_norm(xl, norm1_g[layer]), ml[0], ml[1])
        if kind == 0:
            lam_init = 0.8 - 0.6 * math.exp(-0.3 * layer)
            lam = (jnp.exp(jnp.sum(attn_lambda_q1[j].astype(F32) * attn_lambda_k1[j].astype(F32)))
                   - jnp.exp(jnp.sum(attn_lambda_q2[j].astype(F32) * attn_lambda_k2[j].astype(F32)))
                   + lam_init)
            qc, kc, vc = attn_qkv(hc, attn_w_qkv[j], attn_q_norm[j], attn_k_norm[j])
            ctx_k.append(kc)
            ctx_v.append(vc)
            yc = blocked_diff_attn(qc, kc, vc, lam, attn_subln_g[j], lam_init) @ attn_w_o[j]
            ql, kl, vl = attn_qkv(hl, attn_w_qkv[j], attn_q_norm[j], attn_k_norm[j])
            ql = apply_rope(ql, cos, sin)
            kl = apply_rope(kl, cos, sin)
            k_all = jnp.concatenate([cache_k[:, j], kl], axis=1)
            v_all = jnp.concatenate([cache_v[:, j], vl], axis=1)
            yl = blocked_diff_attn(ql, k_all, v_all, lam, attn_subln_g[j], lam_init) @ attn_w_o[j]
        elif kind == 1:
            yc = pool_mixer(hc, pool_w[j], pool_scale[j])
            yl = pool_mixer(hl, pool_w[j], pool_scale[j])
        else:
            yc = gmlp_mixer(hc, gmlp_w_in[j], gmlp_ln_g[j], gmlp_ln_b[j], gmlp_w_s[j], gmlp_b_s[j], gmlp_w_out[j])
            yl = gmlp_mixer(hl, gmlp_w_in[j], gmlp_ln_g[j], gmlp_ln_b[j], gmlp_w_s[j], gmlp_b_s[j], gmlp_w_out[j])
        xc = xc + mc[2] * yc
        xl = xl + ml[2] * yl
        hc = modulate(rms_norm(xc, norm2_g[layer]), mc[3], mc[4])
        hl = modulate(rms_norm(xl, norm2_g[layer]), ml[3], ml[4])
        xc = xc + mc[5] * moe(hc, moe_w_router[layer], moe_b_router[layer], moe_w_gu[layer],
                              moe_b_gu[layer], moe_w_down[layer], moe_b_down[layer])
        xl = xl + ml[5] * moe(hl, moe_w_router[layer], moe_b_router[layer], moe_w_gu[layer],
                              moe_b_gu[layer], moe_w_down[layer], moe_b_down[layer])
    y_prompt = xc
    y_sample = xl
    k_ctx = jnp.stack(ctx_k, axis=1)
    v_ctx = jnp.stack(ctx_v, axis=1)
    return (y_prompt, y_sample, k_ctx, v_ctx)
```

```python
import functools
import math

import jax
import jax.numpy as jnp
from jax import lax
from jax.experimental import pallas as pl
from jax.experimental.pallas import tpu as pltpu

F32 = jnp.float32
BF16 = jnp.bfloat16

D_MODEL = 2048
BATCH = 32
SEQ = 256
DEPTH = 4
DEC_BATCH = 2
DEC_SEQ = 4096
PAST_LEN = 256
GRID_W = 64
EPS = 1e-6
N_HEADS = 8
HEAD_DIM = 128
V_DIM = 2 * HEAD_DIM
ROPE_THETA = 10000.0
POOL_WINDOWS = (2, 4, 8, 16)
POOL_GROUP = D_MODEL // 4
CHUNK = 128
GMLP_WIDTH = 2 * D_MODEL
GMLP_GROUPS = 8
N_EXPERTS = 32
TOP_K = 4
D_EXPERT = D_MODEL
SWIGLU_LIMIT = 7.0
SWIGLU_ALPHA = 1.702

N_CTX = BATCH * SEQ
N_LAT = DEC_BATCH * DEC_SEQ
N_TOK = N_CTX + N_LAT
N_COND = 8
QK_COLS = 2 * N_HEADS * HEAD_DIM

VMEM_LIMIT = 56 * 1024 * 1024
LANES = 128

TM_LIN = 512
TN_LIN = 512
TQ_LAT = 128
TM_POOL = 256
TM_GMLP = 256
TM_MOE = 256
TH_MOE = 1024
TN_MOE = 1024
TM_TOK = 256


def _cparams(n_axes):
    return pltpu.CompilerParams(
        dimension_semantics=("arbitrary",) * n_axes, vmem_limit_bytes=VMEM_LIMIT)


def _cond_of_row_tile(i, tm):
    row = i * tm
    return jnp.where(row < N_CTX, 0, 1 + (row - N_CTX) // DEC_SEQ)


def _rms_mod(x, g, shift, scale):
    xf = x.astype(F32)
    y = xf * lax.rsqrt(jnp.mean(xf * xf, axis=-1, keepdims=True) + EPS)
    return (y * g) * (1.0 + scale) + shift


def _split_bf16(x):
    hi = x.astype(BF16)
    lo = (x - hi.astype(F32)).astype(BF16)
    return hi, lo


def _dot(a, b):
    return jnp.dot(a, b, preferred_element_type=F32)


def _dot3(a, b):
    a_hi, a_lo = _split_bf16(a)
    b_hi, b_lo = _split_bf16(b)
    return _dot(a_hi, b_hi) + (_dot(a_lo, b_hi) + _dot(a_hi, b_lo))


def _modulation_kernel(cond_ref, w_ref, b_ref, o_ref):
    cnd = cond_ref[...]
    act = cnd * (1.0 / (1.0 + jnp.exp(-cnd)))
    o_ref[...] = _dot3(act, w_ref[...]) + b_ref[...]


def _modulation(cond, w_mod, b_mod, tn=1024):
    n_out = w_mod.shape[-1]
    return pl.pallas_call(
        _modulation_kernel,
        out_shape=jax.ShapeDtypeStruct((DEPTH, N_COND, n_out), F32),
        grid=(DEPTH, n_out // tn),
        in_specs=[
            pl.BlockSpec((N_COND, D_MODEL), lambda l, j: (0, 0)),
            pl.BlockSpec((None, D_MODEL, tn), lambda l, j: (l, 0, j)),
            pl.BlockSpec((None, 1, tn), lambda l, j: (l, 0, j)),
        ],
        out_specs=pl.BlockSpec((None, N_COND, tn), lambda l, j: (l, 0, j)),
        compiler_params=_cparams(2),
        name="modulation",
    )(cond, w_mod, b_mod.reshape(DEPTH, 1, n_out))


def _rope_rotate(x):
    lane = lax.broadcasted_iota(jnp.int32, x.shape, 1)
    first_half = (lane % (HEAD_DIM // 2)) < (HEAD_DIM // 4)
    return jnp.where(first_half, -pltpu.roll(x, HEAD_DIM - HEAD_DIM // 4, 1),
                     pltpu.roll(x, HEAD_DIM // 4, 1))


def _linear_kernel(*refs, norm, epi, out_scale, n_out):
    it = iter(refs)
    x_ref = next(it)
    if norm:
        g_ref, shift_ref, scale_ref = next(it), next(it), next(it)
    w_ref = next(it)
    if epi == "qk":
        hg_ref, cos_ref, sin_ref = next(it), next(it), next(it)
    if epi == "residual":
        res_ref, gate_ref = next(it), next(it)
    out_refs = [next(it) for _ in range(n_out)]
    if norm:
        h_ref = next(it)

        @pl.when(pl.program_id(1) == 0)
        def _():
            h_ref[...] = _rms_mod(x_ref[...], g_ref[...], shift_ref[...], scale_ref[...]).astype(BF16)

        a = h_ref[...]
    else:
        a = x_ref[...]
    acc = _dot(a, w_ref[...].astype(BF16))
    if epi == "qk":
        cos, sin, hg = cos_ref[...], sin_ref[...], hg_ref[...]
        for c in range(acc.shape[1] // HEAD_DIM):
            sl = slice(c * HEAD_DIM, (c + 1) * HEAD_DIM)
            p = acc[:, sl]
            qn = p * lax.rsqrt(jnp.mean(p * p, axis=-1, keepdims=True) + EPS) * hg
            r = (qn * cos + _rope_rotate(qn) * sin) * out_scale
            for o_ref in out_refs:
                o_ref[:, sl] = r.astype(o_ref.dtype)
        return
    if epi == "gelu":
        acc = jax.nn.gelu(acc)
    elif epi == "residual":
        acc = res_ref[...] + gate_ref[...] * acc
    for o_ref in out_refs:
        o_ref[...] = acc.astype(o_ref.dtype)


def _linear(x, w, col_off, n_cols, out_dtypes, *, norm=None, epi="plain", qk=None,
            residual=None, out_scale=1.0, tm=TM_LIN, tn=TN_LIN):
    t, k = x.shape
    joff = col_off // tn
    in_specs = [pl.BlockSpec((tm, k), lambda i, j: (i, 0))]
    args = [x]
    if norm is not None:
        g, mod, shift_idx, scale_idx = norm
        in_specs += [
            pl.BlockSpec((1, k), lambda i, j: (0, 0)),
            pl.BlockSpec((None, None, 1, k), lambda i, j: (_cond_of_row_tile(i, tm), shift_idx, 0, 0)),
            pl.BlockSpec((None, None, 1, k), lambda i, j: (_cond_of_row_tile(i, tm), scale_idx, 0, 0)),
        ]
        args += [g, mod, mod]
    in_specs.append(pl.BlockSpec((k, tn), lambda i, j: (0, joff + j)))
    args.append(w)
    if epi == "qk":
        hg, cos, sin = qk
        in_specs += [
            pl.BlockSpec((1, HEAD_DIM), lambda i, j: (0, 0)),
            pl.BlockSpec((tm, HEAD_DIM), lambda i, j: (i, 0)),
            pl.BlockSpec((tm, HEAD_DIM), lambda i, j: (i, 0)),
        ]
        args += [hg, cos, sin]
    if epi == "residual":
        res, mod, gate_idx = residual
        in_specs += [
            pl.BlockSpec((tm, tn), lambda i, j: (i, j)),
            pl.BlockSpec((None, None, 1, tn), lambda i, j: (_cond_of_row_tile(i, tm), gate_idx, 0, j)),
        ]
        args += [res, mod]
    out_shape = [jax.ShapeDtypeStruct((t, n_cols), dt) for dt in out_dtypes]
    out_specs = [pl.BlockSpec((tm, tn), lambda i, j: (i, j)) for _ in out_dtypes]
    scratch = [pltpu.VMEM((tm, k), BF16)] if norm is not None else []
    return pl.pallas_call(
        functools.partial(_linear_kernel, norm=norm is not None, epi=epi, out_scale=out_scale,
                          n_out=len(out_dtypes)),
        out_shape=out_shape,
        grid=(t // tm, n_cols // tn),
        in_specs=in_specs,
        out_specs=out_specs,
        scratch_shapes=scratch,
        compiler_params=_cparams(2),
        name="linear_" + epi,
    )(*args)


def _lambda_value(lam_ref, lam_init):
    l = lam_ref[...]
    s1 = jnp.sum(l[0:1] * l[1:2], axis=-1, keepdims=True)
    s2 = jnp.sum(l[2:3] * l[3:4], axis=-1, keepdims=True)
    return jnp.exp(s1) - jnp.exp(s2) + lam_init


def _nt_dot(a, b):
    return lax.dot_general(a, b, (((1,), (1,)), ((), ())), preferred_element_type=F32)


def _diff_attn_head(q, segs, lam, subg, lam_init):
    probs = []
    for m in range(2):
        qm = q[:, m * HEAD_DIM:(m + 1) * HEAD_DIM]
        s = [_nt_dot(qm, k[:, m * HEAD_DIM:(m + 1) * HEAD_DIM]) for k, _ in segs]
        mx = functools.reduce(jnp.maximum, [jnp.max(si, axis=-1, keepdims=True) for si in s])
        p = [jnp.exp(si - mx) for si in s]
        den = functools.reduce(jnp.add, [jnp.sum(pi, axis=-1, keepdims=True) for pi in p])
        probs.append((p, den))
    r1 = 1.0 / probs[0][1]
    r2 = lam / probs[1][1]
    o = None
    for si, (_, v) in enumerate(segs):
        a = probs[0][0][si] * r1 - probs[1][0][si] * r2
        part = _dot(a.astype(BF16), v)
        o = part if o is None else o + part
    o = o * lax.rsqrt(jnp.mean(o * o, axis=-1, keepdims=True) + EPS)
    return o * subg * (1.0 - lam_init)


def _ctx_attn_kernel(lam_ref, subg_ref, q_ref, k_ref, v_ref, o_ref, *, lam_init):
    lam = _lambda_value(lam_ref, lam_init)
    subg = subg_ref[...]
    for h in range(N_HEADS):
        sl = slice(h * V_DIM, (h + 1) * V_DIM)
        o = _diff_attn_head(q_ref[:, sl], [(k_ref[:, sl], v_ref[:, sl])], lam, subg, lam_init)
        o_ref[:, sl] = o.astype(o_ref.dtype)


def _lat_attn_kernel(lam_ref, subg_ref, q_ref, kc_ref, vc_ref, k_ref, v_ref, o_ref, kcb_ref, vcb_ref,
                     *, lam_init):
    @pl.when(pl.program_id(2) == 0)
    def _():
        kcb_ref[...] = kc_ref[...].astype(BF16)
        vcb_ref[...] = vc_ref[...].astype(BF16)

    lam = _lambda_value(lam_ref, lam_init)
    segs = [(kcb_ref[...], vcb_ref[...]), (k_ref[...], v_ref[...])]
    o = _diff_attn_head(q_ref[...], segs, lam, subg_ref[...], lam_init)
    o_ref[...] = o.astype(o_ref.dtype)


def _attention(q, k, v, cache_k, cache_v, lam_vecs, subg, lam_init):
    small = [pl.BlockSpec((N_COND, HEAD_DIM), lambda *_: (0, 0)),
             pl.BlockSpec((1, V_DIM), lambda *_: (0, 0))]
    o_ctx = pl.pallas_call(
        functools.partial(_ctx_attn_kernel, lam_init=lam_init),
        out_shape=jax.ShapeDtypeStruct((N_CTX, N_HEADS * V_DIM), BF16),
        grid=(BATCH,),
        in_specs=small + [pl.BlockSpec((SEQ, N_HEADS * V_DIM), lambda b: (b, 0))] * 3,
        out_specs=pl.BlockSpec((SEQ, N_HEADS * V_DIM), lambda b: (b, 0)),
        compiler_params=_cparams(1),
        name="attn_context",
    )(lam_vecs, subg, q, k, v)
    nq = DEC_SEQ // TQ_LAT
    q0 = N_CTX // TQ_LAT
    kv0 = N_CTX // DEC_SEQ
    o_lat = pl.pallas_call(
        functools.partial(_lat_attn_kernel, lam_init=lam_init),
        out_shape=jax.ShapeDtypeStruct((N_LAT, N_HEADS * V_DIM), BF16),
        grid=(DEC_BATCH, N_HEADS, nq),
        in_specs=small + [
            pl.BlockSpec((TQ_LAT, V_DIM), lambda b, h, i: (q0 + b * nq + i, h)),
            pl.BlockSpec((PAST_LEN, V_DIM), lambda b, h, i: (b, h)),
            pl.BlockSpec((PAST_LEN, V_DIM), lambda b, h, i: (b, h)),
            pl.BlockSpec((DEC_SEQ, V_DIM), lambda b, h, i: (kv0 + b, h)),
            pl.BlockSpec((DEC_SEQ, V_DIM), lambda b, h, i: (kv0 + b, h)),
        ],
        out_specs=pl.BlockSpec((TQ_LAT, V_DIM), lambda b, h, i: (b * nq + i, h)),
        scratch_shapes=[pltpu.VMEM((PAST_LEN, V_DIM), BF16), pltpu.VMEM((PAST_LEN, V_DIM), BF16)],
        compiler_params=_cparams(3),
        name="attn_latent",
    )(lam_vecs, subg, q, cache_k, cache_v, k, v)
    return jnp.concatenate([o_ctx, o_lat], axis=0)


def _pool_kernel(x_ref, xp_ref, xn_ref, g_ref, shift_ref, scale_ref, gate_ref, w_ref, ps_ref, o_ref):
    i = pl.program_id(0)
    tm = x_ref.shape[0]
    row0 = i * tm
    seq_pos = jnp.where(row0 < N_CTX, row0 % SEQ, (row0 - N_CTX) % DEC_SEQ)
    seq_len = jnp.where(row0 < N_CTX, SEQ, DEC_SEQ)
    has_prev = seq_pos > 0
    has_next = seq_pos + tm < seq_len
    g, shift, scale = g_ref[...], shift_ref[...], scale_ref[...]
    x = x_ref[...]
    h = _rms_mod(x, g, shift, scale)
    halo = xp_ref.shape[0]
    hp = _rms_mod(xp_ref[...], g, shift, scale) * has_prev.astype(F32)
    hn = _rms_mod(xn_ref[...], g, shift, scale) * has_next.astype(F32)
    t_c = lax.broadcasted_iota(jnp.int32, (tm, tm), 0)
    s_c = lax.broadcasted_iota(jnp.int32, (tm, tm), 1)
    t_h = lax.broadcasted_iota(jnp.int32, (tm, halo), 0)
    s_h = lax.broadcasted_iota(jnp.int32, (tm, halo), 1)
    t_1 = lax.broadcasted_iota(jnp.int32, (tm, 1), 0)
    for grp, win in enumerate(POOL_WINDOWS):
        half = win // 2
        sl = slice(grp * POOL_GROUP, (grp + 1) * POOL_GROUP)
        band_c = ((s_c >= t_c - half) & (s_c < t_c + half)).astype(BF16)
        band_p = (s_h - halo >= t_h - half).astype(BF16)
        band_n = (s_h + tm < t_h + half).astype(BF16)
        tot = None
        for band, src in ((band_c, h[:, sl]), (band_p, hp[:, sl]), (band_n, hn[:, sl])):
            hi, lo = _split_bf16(src)
            part = _dot(band, hi) + _dot(band, lo)
            tot = part if tot is None else tot + part
        lo_edge = jnp.where(has_prev, t_1 - half, jnp.maximum(t_1 - half, 0))
        hi_edge = jnp.where(has_next, t_1 + half, jnp.minimum(t_1 + half, tm))
        diff = tot / (hi_edge - lo_edge).astype(F32) - h[:, sl]
        y = _dot(diff.astype(BF16), w_ref[grp].astype(BF16)) * ps_ref[:, sl]
        o_ref[:, sl] = x[:, sl] + gate_ref[:, sl] * y


def _pool_mixer(x, g, mod, pool_w, pool_scale, tm=TM_POOL, halo=8):
    t, d = x.shape
    nb = tm // halo
    last = t // halo - 1
    modspec = lambda idx: pl.BlockSpec((None, None, 1, d), lambda i: (_cond_of_row_tile(i, tm), idx, 0, 0))
    return pl.pallas_call(
        _pool_kernel,
        out_shape=jax.ShapeDtypeStruct((t, d), F32),
        grid=(t // tm,),
        in_specs=[
            pl.BlockSpec((tm, d), lambda i: (i, 0)),
            pl.BlockSpec((halo, d), lambda i: (jnp.maximum(i * nb - 1, 0), 0)),
            pl.BlockSpec((halo, d), lambda i: (jnp.minimum((i + 1) * nb, last), 0)),
            pl.BlockSpec((1, d), lambda i: (0, 0)),
            modspec(0), modspec(1), modspec(2),
            pl.BlockSpec((len(POOL_WINDOWS), POOL_GROUP, POOL_GROUP), lambda i: (0, 0, 0)),
            pl.BlockSpec((1, d), lambda i: (0, 0)),
        ],
        out_specs=pl.BlockSpec((tm, d), lambda i: (i, 0)),
        compiler_params=_cparams(1),
        name="pool_mixer",
    )(x, x, x, g, mod, mod, mod, pool_w, pool_scale)


def _gmlp_out_kernel(u_ref, v_ref, lg_ref, lb_ref, ws_ref, bs_ref, w_ref, res_ref, gate_ref, o_ref, a_ref):
    @pl.when(pl.program_id(1) == 0)
    def _():
        v = v_ref[...].astype(F32)
        mu = jnp.mean(v, axis=-1, keepdims=True)
        vc = v - mu
        vn = vc * lax.rsqrt(jnp.mean(vc * vc, axis=-1, keepdims=True) + EPS) * lg_ref[...] + lb_ref[...]
        vn = vn.astype(BF16)
        gw = GMLP_WIDTH // GMLP_GROUPS
        for c in range(v.shape[0] // CHUNK):
            rows = slice(c * CHUNK, (c + 1) * CHUNK)
            for grp in range(GMLP_GROUPS):
                cols = slice(grp * gw, (grp + 1) * gw)
                mixed = _dot(ws_ref[grp].astype(BF16), vn[rows, cols]) + bs_ref[:, grp:grp + 1]
                a_ref[rows, cols] = (u_ref[rows, cols].astype(F32) * mixed).astype(BF16)

    acc = _dot(a_ref[...], w_ref[...].astype(BF16))
    o_ref[...] = res_ref[...] + gate_ref[...] * acc


def _gmlp_out(z, ln_g, ln_b, w_s, b_s_t, w_out, res, mod, gate_idx, tm=TM_GMLP, tn=TN_LIN):
    t = z.shape[0]
    d = w_out.shape[1]
    return pl.pallas_call(
        _gmlp_out_kernel,
        out_shape=jax.ShapeDtypeStruct((t, d), F32),
        grid=(t // tm, d // tn),
        in_specs=[
            pl.BlockSpec((tm, GMLP_WIDTH), lambda i, j: (i, 0)),
            pl.BlockSpec((tm, GMLP_WIDTH), lambda i, j: (i, 1)),
            pl.BlockSpec((1, GMLP_WIDTH), lambda i, j: (0, 0)),
            pl.BlockSpec((1, GMLP_WIDTH), lambda i, j: (0, 0)),
            pl.BlockSpec((GMLP_GROUPS, CHUNK, CHUNK), lambda i, j: (0, 0, 0)),
            pl.BlockSpec((CHUNK, GMLP_GROUPS), lambda i, j: (0, 0)),
            pl.BlockSpec((GMLP_WIDTH, tn), lambda i, j: (0, j)),
            pl.BlockSpec((tm, tn), lambda i, j: (i, j)),
            pl.BlockSpec((None, None, 1, tn), lambda i, j: (_cond_of_row_tile(i, tm), gate_idx, 0, j)),
        ],
        out_specs=pl.BlockSpec((tm, tn), lambda i, j: (i, j)),
        scratch_shapes=[pltpu.VMEM((tm, GMLP_WIDTH), BF16)],
        compiler_params=_cparams(2),
        name="gmlp_out",
    )(z, z, ln_g, ln_b, w_s, b_s_t, w_out, res, mod)


def _router_kernel(x_ref, g_ref, shift_ref, scale_ref, w_ref, b_ref, h_ref, idx_ref, gates_ref):
    h = _rms_mod(x_ref[...], g_ref[...], shift_ref[...], scale_ref[...])
    h_ref[...] = h
    logits = _dot3(h, w_ref[...]) + b_ref[...]
    lane = lax.broadcasted_iota(jnp.int32, logits.shape, 1).astype(F32)
    vals, idxs = [], []
    for _ in range(TOP_K):
        m = jnp.max(logits, axis=-1, keepdims=True)
        am = jnp.min(jnp.where(logits == m, lane, float(N_EXPERTS)), axis=-1, keepdims=True)
        vals.append(m)
        idxs.append(am)
        logits = jnp.where(lane == am, -jnp.inf, logits)
    exps = [jnp.exp(v - vals[0]) for v in vals]
    den = functools.reduce(jnp.add, exps)
    out_lane = lax.broadcasted_iota(jnp.int32, idx_ref.shape, 1)
    idx_out = jnp.zeros(idx_ref.shape, F32)
    gate_out = jnp.zeros(gates_ref.shape, F32)
    for r in range(TOP_K):
        idx_out = jnp.where(out_lane == r, idxs[r], idx_out)
        gate_out = jnp.where(out_lane == r, exps[r] / den, gate_out)
    idx_ref[...] = idx_out.astype(jnp.int32)
    gates_ref[...] = gate_out


def _router(x, g, mod, shift_idx, scale_idx, w_router, b_router, tm=TM_TOK):
    t, d = x.shape
    modspec = lambda idx: pl.BlockSpec((None, None, 1, d), lambda i: (_cond_of_row_tile(i, tm), idx, 0, 0))
    return pl.pallas_call(
        _router_kernel,
        out_shape=[jax.ShapeDtypeStruct((t, d), F32),
                   jax.ShapeDtypeStruct((t, LANES), jnp.int32),
                   jax.ShapeDtypeStruct((t, LANES), F32)],
        grid=(t // tm,),
        in_specs=[
            pl.BlockSpec((tm, d), lambda i: (i, 0)),
            pl.BlockSpec((1, d), lambda i: (0, 0)),
            modspec(shift_idx), modspec(scale_idx),
            pl.BlockSpec((d, N_EXPERTS), lambda i: (0, 0)),
            pl.BlockSpec((1, N_EXPERTS), lambda i: (0, 0)),
        ],
        out_specs=[pl.BlockSpec((tm, d), lambda i: (i, 0)),
                   pl.BlockSpec((tm, LANES), lambda i: (i, 0)),
                   pl.BlockSpec((tm, LANES), lambda i: (i, 0))],
        compiler_params=_cparams(1),
        name="moe_router",
    )(x, g, mod, mod, w_router, b_router)


def _expert_changed(te_ref, i):
    return (i == 0) | (te_ref[i] != te_ref[jnp.maximum(i - 1, 0)])


def _expert_up_kernel(te_ref, nu_ref, x_ref, wg_ref, wu_ref, bg_ref, bu_ref, o_ref, wgb_ref, wub_ref):
    i = pl.program_id(1)

    @pl.when(_expert_changed(te_ref, i))
    def _():
        wgb_ref[...] = wg_ref[...].astype(BF16)
        wub_ref[...] = wu_ref[...].astype(BF16)

    @pl.when(i < nu_ref[0])
    def _():
        x = x_ref[...]
        gate = jnp.minimum(_dot(x, wgb_ref[...]) + bg_ref[...], SWIGLU_LIMIT)
        up = jnp.clip(_dot(x, wub_ref[...]) + bu_ref[...], -SWIGLU_LIMIT, SWIGLU_LIMIT)
        act = gate * (1.0 / (1.0 + jnp.exp(-SWIGLU_ALPHA * gate))) * (up + 1.0)
        o_ref[...] = act.astype(o_ref.dtype)

    @pl.when(i >= nu_ref[0])
    def _():
        o_ref[...] = jnp.zeros(o_ref.shape, o_ref.dtype)


def _expert_down_kernel(te_ref, nu_ref, a_ref, w_ref, b_ref, o_ref, wb_ref):
    i = pl.program_id(1)

    @pl.when(_expert_changed(te_ref, i))
    def _():
        wb_ref[...] = w_ref[...].astype(BF16)

    @pl.when(i < nu_ref[0])
    def _():
        o_ref[...] = _dot(a_ref[...], wb_ref[...]) + b_ref[...]

    @pl.when(i >= nu_ref[0])
    def _():
        o_ref[...] = jnp.zeros(o_ref.shape, o_ref.dtype)


def _experts(xs, tile_expert, n_used, w_gu, b_gu, w_down, b_down, tm=TM_MOE, th=TH_MOE, tn=TN_MOE):
    r, d = xs.shape
    n_tiles = r // tm
    n_j = D_EXPERT // th
    row = lambda i, nu: jnp.minimum(i, jnp.maximum(nu[0] - 1, 0))
    act = pl.pallas_call(
        _expert_up_kernel,
        out_shape=jax.ShapeDtypeStruct((r, D_EXPERT), BF16),
        grid_spec=pltpu.PrefetchScalarGridSpec(
            num_scalar_prefetch=2,
            grid=(n_j, n_tiles),
            in_specs=[
                pl.BlockSpec((tm, d), lambda j, i, te, nu: (row(i, nu), 0)),
                pl.BlockSpec((None, d, th), lambda j, i, te, nu: (te[i], 0, j)),
                pl.BlockSpec((None, d, th), lambda j, i, te, nu: (te[i], 0, n_j + j)),
                pl.BlockSpec((None, 1, th), lambda j, i, te, nu: (te[i], 0, j)),
                pl.BlockSpec((None, 1, th), lambda j, i, te, nu: (te[i], 0, n_j + j)),
            ],
            out_specs=pl.BlockSpec((tm, th), lambda j, i, te, nu: (i, j)),
            scratch_shapes=[pltpu.VMEM((d, th), BF16), pltpu.VMEM((d, th), BF16)],
        ),
        compiler_params=_cparams(2),
        name="moe_expert_up",
    )(tile_expert, n_used, xs, w_gu, w_gu, b_gu, b_gu)
    n_n = d // tn
    return pl.pallas_call(
        _expert_down_kernel,
        out_shape=jax.ShapeDtypeStruct((r, d), F32),
        grid_spec=pltpu.PrefetchScalarGridSpec(
            num_scalar_prefetch=2,
            grid=(n_n, n_tiles),
            in_specs=[
                pl.BlockSpec((tm, D_EXPERT), lambda n, i, te, nu: (row(i, nu), 0)),
                pl.BlockSpec((None, D_EXPERT, tn), lambda n, i, te, nu: (te[i], 0, n)),
                pl.BlockSpec((None, 1, tn), lambda n, i, te, nu: (te[i], 0, n)),
            ],
            out_specs=pl.BlockSpec((tm, tn), lambda n, i, te, nu: (i, n)),
            scratch_shapes=[pltpu.VMEM((D_EXPERT, tn), BF16)],
        ),
        compiler_params=_cparams(2),
        name="moe_expert_down",
    )(tile_expert, n_used, act, w_down, b_down)


def _combine_kernel(x_ref, y_ref, gates_ref, gate_ref, o_ref):
    gates = gates_ref[...]
    acc = None
    for kk in range(TOP_K):
        part = gates[:, kk:kk + 1] * y_ref[kk]
        acc = part if acc is None else acc + part
    o_ref[...] = x_ref[...] + gate_ref[...] * acc


def _combine(x, yg, gates, mod, gate_idx, tm=TM_TOK):
    t, d = x.shape
    return pl.pallas_call(
        _combine_kernel,
        out_shape=jax.ShapeDtypeStruct((t, d), F32),
        grid=(t // tm,),
        in_specs=[
            pl.BlockSpec((tm, d), lambda i: (i, 0)),
            pl.BlockSpec((TOP_K, tm, d), lambda i: (0, i, 0)),
            pl.BlockSpec((tm, LANES), lambda i: (i, 0)),
            pl.BlockSpec((None, None, 1, d), lambda i: (_cond_of_row_tile(i, tm), gate_idx, 0, 0)),
        ],
        out_specs=pl.BlockSpec((tm, d), lambda i: (i, 0)),
        compiler_params=_cparams(1),
        name="moe_combine",
    )(x, yg, gates, mod)


def _dispatch_plan(top_idx, tm=TM_MOE):
    n_assign = top_idx.shape[0] * TOP_K
    flat_e = top_idx.reshape(-1)
    onehot = (flat_e[:, None] == jnp.arange(N_EXPERTS, dtype=jnp.int32)[None, :]).astype(jnp.int32)
    csum = jnp.cumsum(onehot, axis=0)
    rank = jnp.take_along_axis(csum, flat_e[:, None], axis=1)[:, 0] - 1
    counts = csum[-1]
    padded = (counts + tm - 1) // tm * tm
    pend = jnp.cumsum(padded)
    pstart = pend - padded
    dest = pstart[flat_e] + rank
    n_tiles = n_assign // tm + N_EXPERTS
    rows_tok = jnp.zeros((n_tiles * tm,), jnp.int32).at[dest].set(
        jnp.arange(n_assign, dtype=jnp.int32) // TOP_K)
    tile_start = jnp.arange(n_tiles, dtype=jnp.int32) * tm
    tile_expert = jnp.minimum(jnp.searchsorted(pend, tile_start, side="right"), N_EXPERTS - 1)
    n_used = (pend[-1] // tm).astype(jnp.int32).reshape(1)
    return dest, rows_tok, tile_expert.astype(jnp.int32), n_used


def _moe_block(x, g, mod, w_router, b_router, w_gu, b_gu, w_down, b_down):
    h, idx, gates = _router(x, g, mod, 3, 4, w_router, b_router.reshape(1, N_EXPERTS))
    dest, rows_tok, tile_expert, n_used = _dispatch_plan(idx[:, :TOP_K])
    xs = jnp.take(h, rows_tok, axis=0).astype(BF16)
    y = _experts(xs, tile_expert, n_used, w_gu, b_gu.reshape(N_EXPERTS, 1, -1),
                 w_down, b_down.reshape(N_EXPERTS, 1, -1))
    yg = jnp.take(y, dest.reshape(-1, TOP_K).T.reshape(-1), axis=0).reshape(TOP_K, x.shape[0], x.shape[1])
    return _combine(x, yg, gates, mod, 5)


def _rope_tables():
    t = jnp.arange(DEC_SEQ)
    row = (t // GRID_W).astype(F32)
    col = (t % GRID_W).astype(F32)
    n_freq = HEAD_DIM // 4
    inv = ROPE_THETA ** (-jnp.arange(n_freq, dtype=F32) / n_freq)
    ang = jnp.concatenate([row[:, None] * inv] * 2 + [col[:, None] * inv] * 2, axis=1)
    cos = jnp.concatenate([jnp.ones((N_CTX, HEAD_DIM), F32)] + [jnp.cos(ang)] * DEC_BATCH, axis=0)
    sin = jnp.concatenate([jnp.zeros((N_CTX, HEAD_DIM), F32)] + [jnp.sin(ang)] * DEC_BATCH, axis=0)
    return cos, sin


def kernel(x_prompt, x_sample, c, cache_k, cache_v, c_ctx, w_mod, b_mod, norm1_g, norm2_g, attn_w_qkv, attn_q_norm, attn_k_norm, attn_lambda_q1, attn_lambda_k1, attn_lambda_q2, attn_lambda_k2, attn_subln_g, attn_w_o, pool_w, pool_scale, gmlp_w_in, gmlp_ln_g, gmlp_ln_b, gmlp_w_s, gmlp_b_s, gmlp_w_out, moe_w_router, moe_b_router, moe_w_gu, moe_b_gu, moe_w_down, moe_b_down):
    d = D_MODEL
    x = jnp.concatenate([x_prompt.reshape(N_CTX, d), x_sample.reshape(N_LAT, d)], axis=0)
    cond = jnp.concatenate([c_ctx[None, :], c, jnp.zeros((N_COND - 1 - DEC_BATCH, d), F32)], axis=0)
    mod_all = _modulation(cond, w_mod, b_mod)
    cos, sin = _rope_tables()
    ctx_k, ctx_v = [], []
    for layer in range(DEPTH):
        kind, j = layer % 3, layer // 3
        mod = mod_all[layer].reshape(N_COND, 6, 1, d)
        g1 = norm1_g[layer].reshape(1, d)
        norm1 = (g1, mod, 0, 1)
        if kind == 0:
            lam_init = 0.8 - 0.6 * math.exp(-0.3 * layer)
            w = attn_w_qkv[j]
            (q,) = _linear(x, w, 0, QK_COLS, [BF16], norm=norm1, epi="qk",
                           qk=(attn_q_norm[j].reshape(1, HEAD_DIM), cos, sin), out_scale=HEAD_DIM ** -0.5)
            k32, k16 = _linear(x, w, QK_COLS, QK_COLS, [F32, BF16], norm=norm1, epi="qk",
                               qk=(attn_k_norm[j].reshape(1, HEAD_DIM), cos, sin))
            v32, v16 = _linear(x, w, 2 * QK_COLS, N_HEADS * V_DIM, [F32, BF16], norm=norm1)
            ctx_k.append(k32[:N_CTX].reshape(BATCH, SEQ, N_HEADS, 2, HEAD_DIM))
            ctx_v.append(v32[:N_CTX].reshape(BATCH, SEQ, N_HEADS, V_DIM))
            lam_vecs = jnp.concatenate(
                [attn_lambda_q1[j][None], attn_lambda_k1[j][None], attn_lambda_q2[j][None],
                 attn_lambda_k2[j][None], jnp.zeros((N_COND - 4, HEAD_DIM), F32)], axis=0)
            o = _attention(q, k16, v16,
                           cache_k[:, j].reshape(DEC_BATCH * PAST_LEN, QK_COLS),
                           cache_v[:, j].reshape(DEC_BATCH * PAST_LEN, N_HEADS * V_DIM),
                           lam_vecs, attn_subln_g[j].reshape(1, V_DIM), lam_init)
            (x,) = _linear(o, attn_w_o[j], 0, d, [F32], epi="residual", residual=(x, mod, 2))
        elif kind == 1:
            x = _pool_mixer(x, g1, mod, pool_w[j], pool_scale[j].reshape(1, d))
        else:
            (z,) = _linear(x, gmlp_w_in[j], 0, 2 * GMLP_WIDTH, [BF16], norm=norm1, epi="gelu")
            x = _gmlp_out(z, gmlp_ln_g[j].reshape(1, -1), gmlp_ln_b[j].reshape(1, -1), gmlp_w_s[j],
                          gmlp_b_s[j].T, gmlp_w_out[j], x, mod, 2)
        x = _moe_block(x, norm2_g[layer].reshape(1, d), mod, moe_w_router[layer], moe_b_router[layer],
                       moe_w_gu[layer], moe_b_gu[layer], moe_w_down[layer], moe_b_down[layer])
    y_prompt = x[:N_CTX].reshape(BATCH, SEQ, d)
    y_sample = x[N_CTX:].reshape(DEC_BATCH, DEC_SEQ, d)
    return (y_prompt, y_sample, jnp.stack(ctx_k, axis=1), jnp.stack(ctx_v, axis=1))
```

```python
import functools
import math

import jax
import jax.numpy as jnp
from jax import lax
from jax.experimental import pallas as pl
from jax.experimental.pallas import tpu as pltpu

F32 = jnp.float32
BF16 = jnp.bfloat16

D_MODEL = 2048
BATCH = 32
SEQ = 256
DEPTH = 4
DEC_BATCH = 2
DEC_SEQ = 4096
PAST_LEN = 256
GRID_W = 64
EPS = 1e-6
N_HEADS = 8
HEAD_DIM = 128
V_DIM = 2 * HEAD_DIM
ROPE_THETA = 10000.0
POOL_WINDOWS = (2, 4, 8, 16)
POOL_GROUP = D_MODEL // 4
CHUNK = 128
GMLP_WIDTH = 2 * D_MODEL
GMLP_GROUPS = 8
N_EXPERTS = 32
TOP_K = 4
D_EXPERT = D_MODEL
SWIGLU_LIMIT = 7.0
SWIGLU_ALPHA = 1.702

N_CTX = BATCH * SEQ
N_LAT = DEC_BATCH * DEC_SEQ
N_TOK = N_CTX + N_LAT
N_COND = 8
QK_COLS = 2 * N_HEADS * HEAD_DIM

VMEM_LIMIT = 56 * 1024 * 1024
LANES = 128
SLABS = D_MODEL // LANES

TM_LIN = 512
TN_LIN = 1024
TQ_LAT = 256
TM_POOL = 256
TM_GMLP = 256
TM_MOE = 256
TH_MOE = 1024
TM_TOK = 256
N_ASSIGN = N_TOK * TOP_K
N_MOE_TILES = N_ASSIGN // TM_MOE + N_EXPERTS
N_MOE_ROWS = N_MOE_TILES * TM_MOE
N_PAD_SLOTS = N_EXPERTS * TM_MOE
PADS_PER_STEP = N_PAD_SLOTS // (N_TOK // TM_TOK)


def _cparams(n_axes):
    return pltpu.CompilerParams(
        dimension_semantics=("arbitrary",) * n_axes, vmem_limit_bytes=VMEM_LIMIT)


def _cond_of_row_tile(i, tm):
    row = i * tm
    return jnp.where(row < N_CTX, 0, 1 + (row - N_CTX) // DEC_SEQ)


def _mod_spec(layer, idx, tm, width, col_axis=None):
    def index_map(*ids):
        i = ids[0]
        jcol = ids[col_axis] if col_axis is not None else 0
        return (layer, _cond_of_row_tile(i, tm), idx, 0, jcol)
    return pl.BlockSpec((None, None, None, 1, width), index_map)


def _rms_mod(x, g, shift, scale):
    xf = x.astype(F32)
    y = xf * lax.rsqrt(jnp.mean(xf * xf, axis=-1, keepdims=True) + EPS)
    return (y * g) * (1.0 + scale) + shift


def _split_bf16(x):
    hi = x.astype(BF16)
    lo = (x - hi.astype(F32)).astype(BF16)
    return hi, lo


def _dot(a, b):
    return jnp.dot(a, b, preferred_element_type=F32)


def _dot3(a, b):
    a_hi, a_lo = _split_bf16(a)
    b_hi, b_lo = _split_bf16(b)
    return _dot(a_hi, b_hi) + (_dot(a_lo, b_hi) + _dot(a_hi, b_lo))


def _modulation_kernel(cond_ref, w_ref, b_ref, o_ref):
    cnd = cond_ref[...]
    act = cnd * (1.0 / (1.0 + jnp.exp(-cnd)))
    o_ref[...] = _dot3(act, w_ref[...]) + b_ref[...]


def _modulation(cond, w_mod, b_mod, tn=1024):
    n_out = w_mod.shape[-1]
    return pl.pallas_call(
        _modulation_kernel,
        out_shape=jax.ShapeDtypeStruct((DEPTH, N_COND, n_out), F32),
        grid=(DEPTH, n_out // tn),
        in_specs=[
            pl.BlockSpec((N_COND, D_MODEL), lambda l, j: (0, 0)),
            pl.BlockSpec((None, D_MODEL, tn), lambda l, j: (l, 0, j)),
            pl.BlockSpec((None, 1, tn), lambda l, j: (l, 0, j)),
        ],
        out_specs=pl.BlockSpec((None, N_COND, tn), lambda l, j: (l, 0, j)),
        compiler_params=_cparams(2),
        name="modulation",
    )(cond, w_mod, b_mod.reshape(DEPTH, 1, n_out))


def _rope_rotate(x):
    lane = lax.broadcasted_iota(jnp.int32, x.shape, 1)
    first_half = (lane % (HEAD_DIM // 2)) < (HEAD_DIM // 4)
    return jnp.where(first_half, -pltpu.roll(x, HEAD_DIM - HEAD_DIM // 4, 1),
                     pltpu.roll(x, HEAD_DIM // 4, 1))


def _linear_kernel(*refs, norm, epi, out_scale, n_out):
    it = iter(refs)
    x_ref = next(it)
    if norm:
        g_ref, shift_ref, scale_ref = next(it), next(it), next(it)
    w_ref = next(it)
    if epi == "qk":
        hg_ref, cos_ref, sin_ref = next(it), next(it), next(it)
    if epi == "residual":
        res_ref, gate_ref = next(it), next(it)
    out_refs = [next(it) for _ in range(n_out)]
    if norm:
        h_ref = next(it)

        @pl.when(pl.program_id(1) == 0)
        def _():
            h_ref[...] = _rms_mod(x_ref[...], g_ref[...], shift_ref[...], scale_ref[...]).astype(BF16)

        a = h_ref[...]
    else:
        a = x_ref[...]
    acc = _dot(a, w_ref[...])
    if epi == "qk":
        cos, sin, hg = cos_ref[...], sin_ref[...], hg_ref[...]
        for c in range(acc.shape[1] // HEAD_DIM):
            sl = slice(c * HEAD_DIM, (c + 1) * HEAD_DIM)
            p = acc[:, sl]
            qn = p * lax.rsqrt(jnp.mean(p * p, axis=-1, keepdims=True) + EPS) * hg
            r = (qn * cos + _rope_rotate(qn) * sin) * out_scale
            for o_ref in out_refs:
                o_ref[:, sl] = r.astype(o_ref.dtype)
        return
    if epi == "gelu":
        acc = jax.nn.gelu(acc)
    elif epi == "residual":
        acc = res_ref[...] + gate_ref[...] * acc
    for o_ref in out_refs:
        o_ref[...] = acc.astype(o_ref.dtype)


def _linear(x, w, layer, col_off, n_cols, out_dtypes, *, norm=None, epi="plain", qk=None,
            residual=None, out_scale=1.0, tm=TM_LIN, tn=TN_LIN):
    t, k = x.shape
    joff = col_off // tn
    in_specs = [pl.BlockSpec((tm, k), lambda i, j: (i, 0))]
    args = [x]
    if norm is not None:
        g, mod, mod_layer, shift_idx, scale_idx = norm
        in_specs += [pl.BlockSpec((1, k), lambda i, j: (0, 0)),
                     _mod_spec(mod_layer, shift_idx, tm, k), _mod_spec(mod_layer, scale_idx, tm, k)]
        args += [g, mod, mod]
    in_specs.append(pl.BlockSpec((None, k, tn), lambda i, j: (layer, 0, joff + j)))
    args.append(w)
    if epi == "qk":
        hg, cos, sin = qk
        in_specs += [
            pl.BlockSpec((1, HEAD_DIM), lambda i, j: (0, 0)),
            pl.BlockSpec((tm, HEAD_DIM), lambda i, j: (i, 0)),
            pl.BlockSpec((tm, HEAD_DIM), lambda i, j: (i, 0)),
        ]
        args += [hg, cos, sin]
    if epi == "residual":
        res, mod, mod_layer, gate_idx = residual
        in_specs += [pl.BlockSpec((tm, tn), lambda i, j: (i, j)),
                     _mod_spec(mod_layer, gate_idx, tm, tn, col_axis=1)]
        args += [res, mod]
    out_shape = [jax.ShapeDtypeStruct((t, n_cols), dt) for dt in out_dtypes]
    out_specs = [pl.BlockSpec((tm, tn), lambda i, j: (i, j)) for _ in out_dtypes]
    scratch = [pltpu.VMEM((tm, k), BF16)] if norm is not None else []
    return pl.pallas_call(
        functools.partial(_linear_kernel, norm=norm is not None, epi=epi, out_scale=out_scale,
                          n_out=len(out_dtypes)),
        out_shape=out_shape,
        grid=(t // tm, n_cols // tn),
        in_specs=in_specs,
        out_specs=out_specs,
        scratch_shapes=scratch,
        compiler_params=_cparams(2),
        name="linear_" + epi,
    )(*args)


def _lambda_value(lam_ref, lam_init):
    l = lam_ref[...]
    s1 = jnp.sum(l[0:1] * l[1:2], axis=-1, keepdims=True)
    s2 = jnp.sum(l[2:3] * l[3:4], axis=-1, keepdims=True)
    return jnp.exp(s1) - jnp.exp(s2) + lam_init


def _nt_dot(a, b):
    return lax.dot_general(a, b, (((1,), (1,)), ((), ())), preferred_element_type=F32)


def _diff_attn_head(q, segs, lam, subg, lam_init):
    maps = []
    for m in range(2):
        qm = q[:, m * HEAD_DIM:(m + 1) * HEAD_DIM]
        s = [_nt_dot(qm, k[:, m * HEAD_DIM:(m + 1) * HEAD_DIM]) for k, _ in segs]
        mx = functools.reduce(jnp.maximum, [jnp.max(si, axis=-1, keepdims=True) for si in s])
        acc, den = None, None
        for si, (_, v) in zip(s, segs):
            p = jnp.exp2(si - mx)
            d = jnp.sum(p, axis=-1, keepdims=True)
            part = _dot(p.astype(BF16), v)
            acc = part if acc is None else acc + part
            den = d if den is None else den + d
        maps.append((acc, den))
    o = maps[0][0] * (1.0 / maps[0][1]) - maps[1][0] * (lam / maps[1][1])
    o = o * lax.rsqrt(jnp.mean(o * o, axis=-1, keepdims=True) + EPS)
    return o * subg * (1.0 - lam_init)


def _ctx_attn_kernel(lam_ref, subg_ref, q_ref, k_ref, v_ref, o_ref, *, lam_init):
    lam = _lambda_value(lam_ref, lam_init)
    subg = subg_ref[...]
    for h in range(N_HEADS):
        sl = slice(h * V_DIM, (h + 1) * V_DIM)
        o = _diff_attn_head(q_ref[:, sl], [(k_ref[:, sl], v_ref[:, sl])], lam, subg, lam_init)
        o_ref[:, sl] = o.astype(o_ref.dtype)


def _lat_attn_kernel(lam_ref, subg_ref, q_ref, kc_ref, vc_ref, k_ref, v_ref, o_ref, kcb_ref, vcb_ref,
                     *, lam_init):
    @pl.when(pl.program_id(2) == 0)
    def _():
        kcb_ref[...] = kc_ref[...].astype(BF16)
        vcb_ref[...] = vc_ref[...].astype(BF16)

    lam = _lambda_value(lam_ref, lam_init)
    segs = [(kcb_ref[...], vcb_ref[...]), (k_ref[...], v_ref[...])]
    o = _diff_attn_head(q_ref[...], segs, lam, subg_ref[...], lam_init)
    o_ref[...] = o.astype(o_ref.dtype)


def _attention(q, k, v, cache_k, cache_v, lam_vecs, subg, lam_init):
    small = [pl.BlockSpec((N_COND, HEAD_DIM), lambda *_: (0, 0)),
             pl.BlockSpec((1, V_DIM), lambda *_: (0, 0))]
    o_ctx = pl.pallas_call(
        functools.partial(_ctx_attn_kernel, lam_init=lam_init),
        out_shape=jax.ShapeDtypeStruct((N_CTX, N_HEADS * V_DIM), BF16),
        grid=(BATCH,),
        in_specs=small + [pl.BlockSpec((SEQ, N_HEADS * V_DIM), lambda b: (b, 0))] * 3,
        out_specs=pl.BlockSpec((SEQ, N_HEADS * V_DIM), lambda b: (b, 0)),
        compiler_params=_cparams(1),
        name="attn_context",
    )(lam_vecs, subg, q, k, v)
    nq = DEC_SEQ // TQ_LAT
    q0 = N_CTX // TQ_LAT
    kv0 = N_CTX // DEC_SEQ
    o_lat = pl.pallas_call(
        functools.partial(_lat_attn_kernel, lam_init=lam_init),
        out_shape=jax.ShapeDtypeStruct((N_LAT, N_HEADS * V_DIM), BF16),
        grid=(DEC_BATCH, N_HEADS, nq),
        in_specs=small + [
            pl.BlockSpec((TQ_LAT, V_DIM), lambda b, h, i: (q0 + b * nq + i, h)),
            pl.BlockSpec((PAST_LEN, V_DIM), lambda b, h, i: (b, h)),
            pl.BlockSpec((PAST_LEN, V_DIM), lambda b, h, i: (b, h)),
            pl.BlockSpec((DEC_SEQ, V_DIM), lambda b, h, i: (kv0 + b, h)),
            pl.BlockSpec((DEC_SEQ, V_DIM), lambda b, h, i: (kv0 + b, h)),
        ],
        out_specs=pl.BlockSpec((TQ_LAT, V_DIM), lambda b, h, i: (b * nq + i, h)),
        scratch_shapes=[pltpu.VMEM((PAST_LEN, V_DIM), BF16), pltpu.VMEM((PAST_LEN, V_DIM), BF16)],
        compiler_params=_cparams(3),
        name="attn_latent",
    )(lam_vecs, subg, q, cache_k, cache_v, k, v)
    return jnp.concatenate([o_ctx, o_lat], axis=0)


def _pool_kernel(x_ref, xp_ref, xn_ref, g_ref, shift_ref, scale_ref, gate_ref, w_ref, ps_ref, o_ref):
    i = pl.program_id(0)
    tm = x_ref.shape[0]
    row0 = i * tm
    seq_pos = jnp.where(row0 < N_CTX, row0 % SEQ, (row0 - N_CTX) % DEC_SEQ)
    seq_len = jnp.where(row0 < N_CTX, SEQ, DEC_SEQ)
    has_prev = seq_pos > 0
    has_next = seq_pos + tm < seq_len
    g, shift, scale = g_ref[...], shift_ref[...], scale_ref[...]
    x = x_ref[...]
    h = _rms_mod(x, g, shift, scale)
    halo = xp_ref.shape[0]
    hp = _rms_mod(xp_ref[...], g, shift, scale) * has_prev.astype(F32)
    hn = _rms_mod(xn_ref[...], g, shift, scale) * has_next.astype(F32)
    t_c = lax.broadcasted_iota(jnp.int32, (tm, tm), 0)
    s_c = lax.broadcasted_iota(jnp.int32, (tm, tm), 1)
    t_h = lax.broadcasted_iota(jnp.int32, (tm, halo), 0)
    s_h = lax.broadcasted_iota(jnp.int32, (tm, halo), 1)
    t_1 = lax.broadcasted_iota(jnp.int32, (tm, 1), 0)
    for grp, win in enumerate(POOL_WINDOWS):
        half = win // 2
        sl = slice(grp * POOL_GROUP, (grp + 1) * POOL_GROUP)
        band_c = ((s_c >= t_c - half) & (s_c < t_c + half)).astype(BF16)
        band_p = (s_h - halo >= t_h - half).astype(BF16)
        band_n = (s_h + tm < t_h + half).astype(BF16)
        tot = None
        for band, src in ((band_c, h[:, sl]), (band_p, hp[:, sl]), (band_n, hn[:, sl])):
            hi, lo = _split_bf16(src)
            part = _dot(band, hi) + _dot(band, lo)
            tot = part if tot is None else tot + part
        lo_edge = jnp.where(has_prev, t_1 - half, jnp.maximum(t_1 - half, 0))
        hi_edge = jnp.where(has_next, t_1 + half, jnp.minimum(t_1 + half, tm))
        diff = tot / (hi_edge - lo_edge).astype(F32) - h[:, sl]
        y = _dot(diff.astype(BF16), w_ref[grp]) * ps_ref[:, sl]
        o_ref[:, sl] = x[:, sl] + gate_ref[:, sl] * y


def _pool_mixer(x, g, mod, layer, pool_w, j, pool_scale, tm=TM_POOL, halo=8):
    t, d = x.shape
    nb = tm // halo
    last = t // halo - 1
    return pl.pallas_call(
        _pool_kernel,
        out_shape=jax.ShapeDtypeStruct((t, d), F32),
        grid=(t // tm,),
        in_specs=[
            pl.BlockSpec((tm, d), lambda i: (i, 0)),
            pl.BlockSpec((halo, d), lambda i: (jnp.maximum(i * nb - 1, 0), 0)),
            pl.BlockSpec((halo, d), lambda i: (jnp.minimum((i + 1) * nb, last), 0)),
            pl.BlockSpec((1, d), lambda i: (0, 0)),
            _mod_spec(layer, 0, tm, d), _mod_spec(layer, 1, tm, d), _mod_spec(layer, 2, tm, d),
            pl.BlockSpec((None, len(POOL_WINDOWS), POOL_GROUP, POOL_GROUP), lambda i: (j, 0, 0, 0)),
            pl.BlockSpec((1, d), lambda i: (0, 0)),
        ],
        out_specs=pl.BlockSpec((tm, d), lambda i: (i, 0)),
        compiler_params=_cparams(1),
        name="pool_mixer",
    )(x, x, x, g, mod, mod, mod, pool_w, pool_scale)


def _gmlp_out_kernel(u_ref, v_ref, lg_ref, lb_ref, ws_ref, bs_ref, w_ref, res_ref, gate_ref, o_ref, a_ref):
    @pl.when(pl.program_id(1) == 0)
    def _():
        v = v_ref[...].astype(F32)
        mu = jnp.mean(v, axis=-1, keepdims=True)
        vc = v - mu
        vn = vc * lax.rsqrt(jnp.mean(vc * vc, axis=-1, keepdims=True) + EPS) * lg_ref[...] + lb_ref[...]
        vn = vn.astype(BF16)
        gw = GMLP_WIDTH // GMLP_GROUPS
        for c in range(v.shape[0] // CHUNK):
            rows = slice(c * CHUNK, (c + 1) * CHUNK)
            for grp in range(GMLP_GROUPS):
                cols = slice(grp * gw, (grp + 1) * gw)
                mixed = _dot(ws_ref[grp], vn[rows, cols]) + bs_ref[:, grp:grp + 1]
                a_ref[rows, cols] = (u_ref[rows, cols].astype(F32) * mixed).astype(BF16)

    acc = _dot(a_ref[...], w_ref[...])
    o_ref[...] = res_ref[...] + gate_ref[...] * acc


def _gmlp_out(z, ln_g, ln_b, w_s, b_s_t, w_out, j, res, mod, layer, gate_idx, tm=TM_GMLP, tn=TN_LIN):
    t = z.shape[0]
    d = w_out.shape[-1]
    return pl.pallas_call(
        _gmlp_out_kernel,
        out_shape=jax.ShapeDtypeStruct((t, d), F32),
        grid=(t // tm, d // tn),
        in_specs=[
            pl.BlockSpec((tm, GMLP_WIDTH), lambda i, n: (i, 0)),
            pl.BlockSpec((tm, GMLP_WIDTH), lambda i, n: (i, 1)),
            pl.BlockSpec((1, GMLP_WIDTH), lambda i, n: (0, 0)),
            pl.BlockSpec((1, GMLP_WIDTH), lambda i, n: (0, 0)),
            pl.BlockSpec((None, GMLP_GROUPS, CHUNK, CHUNK), lambda i, n: (j, 0, 0, 0)),
            pl.BlockSpec((CHUNK, GMLP_GROUPS), lambda i, n: (0, 0)),
            pl.BlockSpec((None, GMLP_WIDTH, tn), lambda i, n: (j, 0, n)),
            pl.BlockSpec((tm, tn), lambda i, n: (i, n)),
            _mod_spec(layer, gate_idx, tm, tn, col_axis=1),
        ],
        out_specs=pl.BlockSpec((tm, tn), lambda i, n: (i, n)),
        scratch_shapes=[pltpu.VMEM((tm, GMLP_WIDTH), BF16)],
        compiler_params=_cparams(2),
        name="gmlp_out",
    )(z, z, ln_g, ln_b, w_s, b_s_t, w_out, res, mod)


def _router_kernel(x_ref, g_ref, shift_ref, scale_ref, w_ref, b_ref, idx_ref, gates_ref, rank_ref, cnt_ref,
                   run_ref):
    tm = x_ref.shape[0]

    @pl.when(pl.program_id(0) == 0)
    def _():
        run_ref[...] = jnp.zeros(run_ref.shape, F32)

    h = _rms_mod(x_ref[...], g_ref[...], shift_ref[...], scale_ref[...])
    logits = _dot3(h, w_ref[...]) + b_ref[...]
    lane = lax.broadcasted_iota(jnp.int32, logits.shape, 1).astype(F32)
    vals, idxs = [], []
    for _ in range(TOP_K):
        m = jnp.max(logits, axis=-1, keepdims=True)
        am = jnp.min(jnp.where(logits == m, lane, float(N_EXPERTS)), axis=-1, keepdims=True)
        vals.append(m)
        idxs.append(am)
        logits = jnp.where(lane == am, -jnp.inf, logits)
    exps = [jnp.exp(v - vals[0]) for v in vals]
    den = functools.reduce(jnp.add, exps)
    wide = lax.broadcasted_iota(jnp.int32, (tm, LANES), 1).astype(F32)
    chosen = functools.reduce(jnp.add, [(wide == e).astype(F32) for e in idxs])
    row = lax.broadcasted_iota(jnp.int32, (tm, tm), 0)
    col = lax.broadcasted_iota(jnp.int32, (tm, tm), 1)
    before = _dot((col < row).astype(BF16), chosen.astype(BF16)) + run_ref[...]
    run_ref[...] = run_ref[...] + jnp.sum(chosen, axis=0, keepdims=True)
    out_lane = lax.broadcasted_iota(jnp.int32, idx_ref.shape, 1)
    idx_out = jnp.zeros(idx_ref.shape, F32)
    gate_out = jnp.zeros(gates_ref.shape, F32)
    rank_out = jnp.zeros(rank_ref.shape, F32)
    for r in range(TOP_K):
        idx_out = jnp.where(out_lane == r, idxs[r], idx_out)
        gate_out = jnp.where(out_lane == r, exps[r] / den, gate_out)
        rank_r = jnp.sum(jnp.where(wide == idxs[r], before, 0.0), axis=-1, keepdims=True)
        rank_out = jnp.where(out_lane == r, rank_r, rank_out)
    idx_ref[...] = idx_out.astype(jnp.int32)
    gates_ref[...] = gate_out
    rank_ref[...] = rank_out.astype(jnp.int32)
    cnt_ref[...] = jnp.broadcast_to(run_ref[...], cnt_ref.shape).astype(jnp.int32)


def _router(x, g, mod, layer, w_router, b_router, tm=TM_TOK):
    t, d = x.shape
    tok_out = lambda: pl.BlockSpec((tm, LANES), lambda i: (i, 0))
    return pl.pallas_call(
        _router_kernel,
        out_shape=[jax.ShapeDtypeStruct((t, LANES), jnp.int32),
                   jax.ShapeDtypeStruct((t, LANES), F32),
                   jax.ShapeDtypeStruct((t, LANES), jnp.int32),
                   jax.ShapeDtypeStruct((N_COND, LANES), jnp.int32)],
        grid=(t // tm,),
        in_specs=[
            pl.BlockSpec((tm, d), lambda i: (i, 0)),
            pl.BlockSpec((1, d), lambda i: (0, 0)),
            _mod_spec(layer, 3, tm, d), _mod_spec(layer, 4, tm, d),
            pl.BlockSpec((None, d, N_EXPERTS), lambda i: (layer, 0, 0)),
            pl.BlockSpec((None, 1, N_EXPERTS), lambda i: (layer, 0, 0)),
        ],
        out_specs=[tok_out(), tok_out(), tok_out(), pl.BlockSpec((N_COND, LANES), lambda i: (0, 0))],
        scratch_shapes=[pltpu.VMEM((1, LANES), F32)],
        compiler_params=_cparams(1),
        name="moe_router",
    )(x, g, mod, mod, w_router, b_router)


def _dispatch_kernel(dest_ref, pad_ref, x_ref, g_ref, shift_ref, scale_ref, xs_ref, h_ref, z_ref, sem, pad_sem):
    i = pl.program_id(0)
    tm = x_ref.shape[0]

    @pl.when(i == 0)
    def _():
        z_ref[...] = jnp.zeros(z_ref.shape, F32)

    h = _rms_mod(x_ref[...], g_ref[...], shift_ref[...], scale_ref[...])
    h_ref[...] = h.reshape(h_ref.shape)
    base = i * (tm * TOP_K)

    def issue(r, carry):
        for kk in range(TOP_K):
            pltpu.make_async_copy(h_ref.at[r], xs_ref.at[dest_ref[base + r * TOP_K + kk]], sem).start()
        return carry

    lax.fori_loop(0, tm, issue, 0)

    def issue_pad(c, carry):
        pltpu.make_async_copy(z_ref.at[c], xs_ref.at[pad_ref[i * PADS_PER_STEP + c]], pad_sem).start()
        return carry

    lax.fori_loop(0, PADS_PER_STEP, issue_pad, 0)
    for _ in range(TOP_K):
        pltpu.make_async_copy(h_ref, xs_ref.at[pl.ds(0, tm)], sem).wait()
    pltpu.make_async_copy(z_ref, xs_ref.at[pl.ds(0, PADS_PER_STEP)], pad_sem).wait()


def _dispatch(x, g, mod, layer, dest, pad_rows, tm=TM_TOK):
    t, d = x.shape
    return pl.pallas_call(
        _dispatch_kernel,
        out_shape=jax.ShapeDtypeStruct((N_MOE_ROWS, SLABS, LANES), F32),
        grid_spec=pltpu.PrefetchScalarGridSpec(
            num_scalar_prefetch=2,
            grid=(t // tm,),
            in_specs=[
                pl.BlockSpec((tm, d), lambda i, *_: (i, 0)),
                pl.BlockSpec((1, d), lambda i, *_: (0, 0)),
                _mod_spec(layer, 3, tm, d), _mod_spec(layer, 4, tm, d),
            ],
            out_specs=pl.BlockSpec(memory_space=pl.ANY),
            scratch_shapes=[pltpu.VMEM((tm, SLABS, LANES), F32), pltpu.VMEM((PADS_PER_STEP, SLABS, LANES), F32),
                            pltpu.SemaphoreType.DMA, pltpu.SemaphoreType.DMA],
        ),
        compiler_params=_cparams(1),
        name="moe_dispatch",
    )(dest, pad_rows, x, g, mod, mod)


def _expert_changed(te_ref, i):
    return (i == 0) | (te_ref[i] != te_ref[jnp.maximum(i - 1, 0)])


def _expert_up_kernel(te_ref, nu_ref, x_ref, wg_ref, wu_ref, bg_ref, bu_ref, o_ref, wgb_ref, wub_ref):
    i = pl.program_id(1)

    @pl.when(_expert_changed(te_ref, i))
    def _():
        wgb_ref[...] = wg_ref[...].astype(BF16)
        wub_ref[...] = wu_ref[...].astype(BF16)

    @pl.when(i < nu_ref[0])
    def _():
        x = x_ref[...].reshape(x_ref.shape[0], D_MODEL).astype(BF16)
        gate = jnp.minimum(_dot(x, wgb_ref[...]) + bg_ref[...], SWIGLU_LIMIT)
        up = jnp.clip(_dot(x, wub_ref[...]) + bu_ref[...], -SWIGLU_LIMIT, SWIGLU_LIMIT)
        act = gate * (1.0 / (1.0 + jnp.exp(-SWIGLU_ALPHA * gate))) * (up + 1.0)
        o_ref[...] = act.astype(o_ref.dtype)

    @pl.when(i >= nu_ref[0])
    def _():
        o_ref[...] = jnp.zeros(o_ref.shape, o_ref.dtype)


def _expert_down_kernel(te_ref, nu_ref, a_ref, w_ref, b_ref, o_ref, wb_ref):
    i = pl.program_id(0)

    @pl.when(_expert_changed(te_ref, i))
    def _():
        wb_ref[...] = w_ref[...].astype(BF16)

    @pl.when(i < nu_ref[0])
    def _():
        y = _dot(a_ref[...], wb_ref[...]) + b_ref[...]
        o_ref[...] = y.reshape(o_ref.shape)

    @pl.when(i >= nu_ref[0])
    def _():
        o_ref[...] = jnp.zeros(o_ref.shape, o_ref.dtype)


def _experts(xs, tile_expert, n_used, layer, w_gu, b_gu, w_down, b_down, tm=TM_MOE, th=TH_MOE):
    d = D_MODEL
    n_j = D_EXPERT // th
    row = lambda i, nu: jnp.minimum(i, jnp.maximum(nu[0] - 1, 0))
    act = pl.pallas_call(
        _expert_up_kernel,
        out_shape=jax.ShapeDtypeStruct((N_MOE_ROWS, D_EXPERT), BF16),
        grid_spec=pltpu.PrefetchScalarGridSpec(
            num_scalar_prefetch=2,
            grid=(n_j, N_MOE_TILES),
            in_specs=[
                pl.BlockSpec((tm, SLABS, LANES), lambda j, i, te, nu: (row(i, nu), 0, 0)),
                pl.BlockSpec((None, None, d, th), lambda j, i, te, nu: (layer, te[i], 0, j)),
                pl.BlockSpec((None, None, d, th), lambda j, i, te, nu: (layer, te[i], 0, n_j + j)),
                pl.BlockSpec((None, None, 1, th), lambda j, i, te, nu: (layer, te[i], 0, j)),
                pl.BlockSpec((None, None, 1, th), lambda j, i, te, nu: (layer, te[i], 0, n_j + j)),
            ],
            out_specs=pl.BlockSpec((tm, th), lambda j, i, te, nu: (i, j)),
            scratch_shapes=[pltpu.VMEM((d, th), BF16), pltpu.VMEM((d, th), BF16)],
        ),
        compiler_params=_cparams(2),
        name="moe_expert_up",
    )(tile_expert, n_used, xs, w_gu, w_gu, b_gu, b_gu)
    return pl.pallas_call(
        _expert_down_kernel,
        out_shape=jax.ShapeDtypeStruct((N_MOE_ROWS, SLABS, LANES), F32),
        grid_spec=pltpu.PrefetchScalarGridSpec(
            num_scalar_prefetch=2,
            grid=(N_MOE_TILES,),
            in_specs=[
                pl.BlockSpec((tm, D_EXPERT), lambda i, te, nu: (row(i, nu), 0)),
                pl.BlockSpec((None, None, D_EXPERT, d), lambda i, te, nu: (layer, te[i], 0, 0)),
                pl.BlockSpec((None, None, 1, d), lambda i, te, nu: (layer, te[i], 0, 0)),
            ],
            out_specs=pl.BlockSpec((tm, SLABS, LANES), lambda i, te, nu: (i, 0, 0)),
            scratch_shapes=[pltpu.VMEM((D_EXPERT, d), BF16)],
        ),
        compiler_params=_cparams(1),
        name="moe_expert_down",
    )(tile_expert, n_used, act, w_down, b_down)


def _combine_kernel(dest_ref, x_ref, gates_ref, gate_ref, y_ref, o_ref, buf_ref, sem):
    i = pl.program_id(0)
    n = pl.num_programs(0)
    tm = x_ref.shape[0]

    def issue(tile, slot):
        base = tile * (tm * TOP_K)

        def body(r, carry):
            for kk in range(TOP_K):
                pltpu.make_async_copy(y_ref.at[dest_ref[base + r * TOP_K + kk]],
                                      buf_ref.at[slot, kk * tm + r], sem.at[slot]).start()
            return carry

        lax.fori_loop(0, tm, body, 0)

    @pl.when(i == 0)
    def _():
        issue(0, 0)

    @pl.when(i + 1 < n)
    def _():
        issue(i + 1, (i + 1) % 2)

    slot = i % 2
    pltpu.make_async_copy(y_ref.at[pl.ds(0, TOP_K * tm)], buf_ref.at[slot], sem.at[slot]).wait()
    gates = gates_ref[...]
    acc = None
    for kk in range(TOP_K):
        part = gates[:, kk:kk + 1] * buf_ref[slot, pl.ds(kk * tm, tm)].reshape(tm, D_MODEL)
        acc = part if acc is None else acc + part
    o_ref[...] = x_ref[...] + gate_ref[...] * acc


def _combine(x, y, gates, dest, mod, layer, tm=TM_TOK):
    t, d = x.shape
    return pl.pallas_call(
        _combine_kernel,
        out_shape=jax.ShapeDtypeStruct((t, d), F32),
        grid_spec=pltpu.PrefetchScalarGridSpec(
            num_scalar_prefetch=1,
            grid=(t // tm,),
            in_specs=[
                pl.BlockSpec((tm, d), lambda i, *_: (i, 0)),
                pl.BlockSpec((tm, LANES), lambda i, *_: (i, 0)),
                _mod_spec(layer, 5, tm, d),
                pl.BlockSpec(memory_space=pl.ANY),
            ],
            out_specs=pl.BlockSpec((tm, d), lambda i, *_: (i, 0)),
            scratch_shapes=[pltpu.VMEM((2, TOP_K * tm, SLABS, LANES), F32), pltpu.SemaphoreType.DMA((2,))],
        ),
        compiler_params=_cparams(1),
        name="moe_combine",
    )(dest, x, gates, mod, y)


def _dispatch_plan(idx, rank, counts, tm=TM_MOE):
    padded = (counts + tm - 1) // tm * tm
    pend = jnp.cumsum(padded)
    pstart = pend - padded
    dest = (jnp.take(pstart, idx.reshape(-1)) + rank.reshape(-1)).astype(jnp.int32)
    tile_start = jnp.arange(N_MOE_TILES, dtype=jnp.int32) * tm
    tile_expert = jnp.minimum(jnp.searchsorted(pend, tile_start, side="right"), N_EXPERTS - 1).astype(jnp.int32)
    n_used = (pend[-1] // tm).astype(jnp.int32).reshape(1)
    seg_len = jnp.concatenate([padded - counts, (N_MOE_ROWS - pend[-1])[None]])
    seg_start = jnp.concatenate([pstart + counts, pend[-1:]])
    seg_end = jnp.cumsum(seg_len)
    slot = jnp.arange(N_PAD_SLOTS, dtype=jnp.int32)
    seg = jnp.searchsorted(seg_end, slot, side="right")
    pad_rows = jnp.take(seg_start, seg) + slot - (jnp.take(seg_end, seg) - jnp.take(seg_len, seg))
    return dest, pad_rows.astype(jnp.int32), tile_expert, n_used


def _moe_block(x, g, mod, layer, w_router, b_router, w_gu, b_gu, w_down, b_down):
    idx, gates, rank, cnt = _router(x, g, mod, layer, w_router, b_router)
    dest, pad_rows, tile_expert, n_used = _dispatch_plan(idx[:, :TOP_K], rank[:, :TOP_K], cnt[0, :N_EXPERTS])
    xs = _dispatch(x, g, mod, layer, dest, pad_rows)
    y = _experts(xs, tile_expert, n_used, layer, w_gu, b_gu, w_down, b_down)
    return _combine(x, y, gates, dest, mod, layer)


def _rope_tables():
    t = jnp.arange(DEC_SEQ)
    row = (t // GRID_W).astype(F32)
    col = (t % GRID_W).astype(F32)
    n_freq = HEAD_DIM // 4
    inv = ROPE_THETA ** (-jnp.arange(n_freq, dtype=F32) / n_freq)
    ang = jnp.concatenate([row[:, None] * inv] * 2 + [col[:, None] * inv] * 2, axis=1)
    cos = jnp.concatenate([jnp.ones((N_CTX, HEAD_DIM), F32)] + [jnp.cos(ang)] * DEC_BATCH, axis=0)
    sin = jnp.concatenate([jnp.zeros((N_CTX, HEAD_DIM), F32)] + [jnp.sin(ang)] * DEC_BATCH, axis=0)
    return cos, sin


def kernel(x_prompt, x_sample, c, cache_k, cache_v, c_ctx, w_mod, b_mod, norm1_g, norm2_g, attn_w_qkv, attn_q_norm, attn_k_norm, attn_lambda_q1, attn_lambda_k1, attn_lambda_q2, attn_lambda_k2, attn_subln_g, attn_w_o, pool_w, pool_scale, gmlp_w_in, gmlp_ln_g, gmlp_ln_b, gmlp_w_s, gmlp_b_s, gmlp_w_out, moe_w_router, moe_b_router, moe_w_gu, moe_b_gu, moe_w_down, moe_b_down):
    d = D_MODEL
    x = jnp.concatenate([x_prompt.reshape(N_CTX, d), x_sample.reshape(N_LAT, d)], axis=0)
    cond = jnp.concatenate([c_ctx[None, :], c, jnp.zeros((N_COND - 1 - DEC_BATCH, d), F32)], axis=0)
    mod = _modulation(cond, w_mod, b_mod).reshape(DEPTH, N_COND, 6, 1, d)
    cos, sin = _rope_tables()
    w_qkv, w_o, w_pool = attn_w_qkv.astype(BF16), attn_w_o.astype(BF16), pool_w.astype(BF16)
    w_in, w_s, w_out = gmlp_w_in.astype(BF16), gmlp_w_s.astype(BF16), gmlp_w_out.astype(BF16)
    b_router = moe_b_router.reshape(DEPTH, 1, N_EXPERTS)
    b_gu = moe_b_gu.reshape(DEPTH, N_EXPERTS, 1, 2 * D_EXPERT)
    b_down = moe_b_down.reshape(DEPTH, N_EXPERTS, 1, d)
    ctx_k, ctx_v = [], []
    for layer in range(DEPTH):
        kind, j = layer % 3, layer // 3
        g1 = norm1_g[layer].reshape(1, d)
        norm1 = (g1, mod, layer, 0, 1)
        if kind == 0:
            lam_init = 0.8 - 0.6 * math.exp(-0.3 * layer)
            (q,) = _linear(x, w_qkv, j, 0, QK_COLS, [BF16], norm=norm1, epi="qk",
                           qk=(attn_q_norm[j].reshape(1, HEAD_DIM), cos, sin),
                           out_scale=HEAD_DIM ** -0.5 * math.log2(math.e))
            k32, k16 = _linear(x, w_qkv, j, QK_COLS, QK_COLS, [F32, BF16], norm=norm1, epi="qk",
                               qk=(attn_k_norm[j].reshape(1, HEAD_DIM), cos, sin))
            v32, v16 = _linear(x, w_qkv, j, 2 * QK_COLS, N_HEADS * V_DIM, [F32, BF16], norm=norm1)
            ctx_k.append(k32[:N_CTX].reshape(BATCH, SEQ, N_HEADS, 2, HEAD_DIM))
            ctx_v.append(v32[:N_CTX].reshape(BATCH, SEQ, N_HEADS, V_DIM))
            lam_vecs = jnp.concatenate(
                [attn_lambda_q1[j][None], attn_lambda_k1[j][None], attn_lambda_q2[j][None],
                 attn_lambda_k2[j][None], jnp.zeros((N_COND - 4, HEAD_DIM), F32)], axis=0)
            o = _attention(q, k16, v16,
                           cache_k[:, j].reshape(DEC_BATCH * PAST_LEN, QK_COLS),
                           cache_v[:, j].reshape(DEC_BATCH * PAST_LEN, N_HEADS * V_DIM),
                           lam_vecs, attn_subln_g[j].reshape(1, V_DIM), lam_init)
            (x,) = _linear(o, w_o, j, 0, d, [F32], epi="residual", residual=(x, mod, layer, 2))
        elif kind == 1:
            x = _pool_mixer(x, g1, mod, layer, w_pool, j, pool_scale[j].reshape(1, d))
        else:
            (z,) = _linear(x, w_in, j, 0, 2 * GMLP_WIDTH, [BF16], norm=norm1, epi="gelu")
            x = _gmlp_out(z, gmlp_ln_g[j].reshape(1, -1), gmlp_ln_b[j].reshape(1, -1), w_s,
                          gmlp_b_s[j].T, w_out, j, x, mod, layer, 2)
        x = _moe_block(x, norm2_g[layer].reshape(1, d), mod, layer, moe_w_router, b_router,
                       moe_w_gu, b_gu, moe_w_down, b_down)
    y_prompt = x[:N_CTX].reshape(BATCH, SEQ, d)
    y_sample = x[N_CTX:].reshape(DEC_BATCH, DEC_SEQ, d)
    return (y_prompt, y_sample, jnp.stack(ctx_k, axis=1), jnp.stack(ctx_v, axis=1))
```

```python
import functools
import math

import jax
import jax.numpy as jnp
from jax import lax
from jax.experimental import pallas as pl
from jax.experimental.pallas import tpu as pltpu

F32 = jnp.float32
BF16 = jnp.bfloat16

D_MODEL = 2048
BATCH = 32
SEQ = 256
DEPTH = 4
DEC_BATCH = 2
DEC_SEQ = 4096
PAST_LEN = 256
GRID_W = 64
EPS = 1e-6
N_HEADS = 8
HEAD_DIM = 128
V_DIM = 2 * HEAD_DIM
ROPE_THETA = 10000.0
POOL_WINDOWS = (2, 4, 8, 16)
POOL_GROUP = D_MODEL // 4
CHUNK = 128
GMLP_WIDTH = 2 * D_MODEL
GMLP_GROUPS = 8
N_EXPERTS = 32
TOP_K = 4
D_EXPERT = D_MODEL
SWIGLU_LIMIT = 7.0
SWIGLU_ALPHA = 1.702

N_CTX = BATCH * SEQ
N_LAT = DEC_BATCH * DEC_SEQ
N_TOK = N_CTX + N_LAT
N_COND = 8
QK_COLS = 2 * N_HEADS * HEAD_DIM

VMEM_LIMIT = 56 * 1024 * 1024
LANES = 128
SLABS = D_MODEL // LANES

TM_LIN = 512
TN_LIN = 1024
TQ_LAT = 256
KEY_CHUNK = 512
TM_POOL = 256
TM_GMLP = 256
TM_MOE = 256
TH_MOE = 1024
TM_TOK = 256
N_ASSIGN = N_TOK * TOP_K
N_MOE_TILES = N_ASSIGN // TM_MOE + N_EXPERTS
N_MOE_ROWS = N_MOE_TILES * TM_MOE
N_PAD_SLOTS = N_EXPERTS * TM_MOE
PADS_PER_STEP = N_PAD_SLOTS // (N_TOK // TM_TOK)


def _cparams(n_axes):
    return pltpu.CompilerParams(
        dimension_semantics=("arbitrary",) * n_axes, vmem_limit_bytes=VMEM_LIMIT)


def _cond_of_row_tile(i, tm):
    row = i * tm
    return jnp.where(row < N_CTX, 0, 1 + (row - N_CTX) // DEC_SEQ)


def _mod_spec(layer, idx, tm, width, col_axis=None):
    def index_map(*ids):
        i = ids[0]
        jcol = ids[col_axis] if col_axis is not None else 0
        return (layer, _cond_of_row_tile(i, tm), idx, 0, jcol)
    return pl.BlockSpec((None, None, None, 1, width), index_map)


def _rms_mod(x, g, shift, scale):
    xf = x.astype(F32)
    y = xf * lax.rsqrt(jnp.mean(xf * xf, axis=-1, keepdims=True) + EPS)
    return (y * g) * (1.0 + scale) + shift


def _split_bf16(x):
    hi = x.astype(BF16)
    lo = (x - hi.astype(F32)).astype(BF16)
    return hi, lo


def _dot(a, b):
    return jnp.dot(a, b, preferred_element_type=F32)


def _dot3(a, b):
    a_hi, a_lo = _split_bf16(a)
    b_hi, b_lo = _split_bf16(b)
    return _dot(a_hi, b_hi) + (_dot(a_lo, b_hi) + _dot(a_hi, b_lo))


def _modulation_kernel(cond_ref, w_ref, b_ref, o_ref):
    cnd = cond_ref[...]
    act = cnd * (1.0 / (1.0 + jnp.exp(-cnd)))
    o_ref[...] = _dot3(act, w_ref[...]) + b_ref[...]


def _modulation(cond, w_mod, b_mod, tn=1024):
    n_out = w_mod.shape[-1]
    return pl.pallas_call(
        _modulation_kernel,
        out_shape=jax.ShapeDtypeStruct((DEPTH, N_COND, n_out), F32),
        grid=(DEPTH, n_out // tn),
        in_specs=[
            pl.BlockSpec((N_COND, D_MODEL), lambda l, j: (0, 0)),
            pl.BlockSpec((None, D_MODEL, tn), lambda l, j: (l, 0, j)),
            pl.BlockSpec((None, 1, tn), lambda l, j: (l, 0, j)),
        ],
        out_specs=pl.BlockSpec((None, N_COND, tn), lambda l, j: (l, 0, j)),
        compiler_params=_cparams(2),
        name="modulation",
    )(cond, w_mod, b_mod.reshape(DEPTH, 1, n_out))


def _rope_rotate(x):
    lane = lax.broadcasted_iota(jnp.int32, x.shape, 1)
    first_half = (lane % (HEAD_DIM // 2)) < (HEAD_DIM // 4)
    return jnp.where(first_half, -pltpu.roll(x, HEAD_DIM - HEAD_DIM // 4, 1),
                     pltpu.roll(x, HEAD_DIM // 4, 1))


def _linear_kernel(*refs, norm, epi, out_scale, n_out):
    it = iter(refs)
    x_ref = next(it)
    if norm:
        g_ref, shift_ref, scale_ref = next(it), next(it), next(it)
    w_ref = next(it)
    if epi == "qk":
        hg_ref, cos_ref, sin_ref = next(it), next(it), next(it)
    if epi == "residual":
        res_ref, gate_ref = next(it), next(it)
    out_refs = [next(it) for _ in range(n_out)]
    if norm:
        h_ref = next(it)

        @pl.when(pl.program_id(1) == 0)
        def _():
            h_ref[...] = _rms_mod(x_ref[...], g_ref[...], shift_ref[...], scale_ref[...]).astype(BF16)

        a = h_ref[...]
    else:
        a = x_ref[...]
    acc = _dot(a, w_ref[...])
    if epi == "qk":
        cos, sin, hg = cos_ref[...], sin_ref[...], hg_ref[...]
        for c in range(acc.shape[1] // HEAD_DIM):
            sl = slice(c * HEAD_DIM, (c + 1) * HEAD_DIM)
            p = acc[:, sl]
            qn = p * lax.rsqrt(jnp.mean(p * p, axis=-1, keepdims=True) + EPS) * hg
            r = (qn * cos + _rope_rotate(qn) * sin) * out_scale
            for o_ref in out_refs:
                o_ref[:, sl] = r.astype(o_ref.dtype)
        return
    if epi == "gelu":
        acc = jax.nn.gelu(acc)
    elif epi == "residual":
        acc = res_ref[...] + gate_ref[...] * acc
    for o_ref in out_refs:
        o_ref[...] = acc.astype(o_ref.dtype)


def _linear(x, w, layer, col_off, n_cols, out_dtypes, *, norm=None, epi="plain", qk=None,
            residual=None, out_scale=1.0, tm=TM_LIN, tn=TN_LIN):
    t, k = x.shape
    joff = col_off // tn
    in_specs = [pl.BlockSpec((tm, k), lambda i, j: (i, 0))]
    args = [x]
    if norm is not None:
        g, mod, mod_layer, shift_idx, scale_idx = norm
        in_specs += [pl.BlockSpec((1, k), lambda i, j: (0, 0)),
                     _mod_spec(mod_layer, shift_idx, tm, k), _mod_spec(mod_layer, scale_idx, tm, k)]
        args += [g, mod, mod]
    in_specs.append(pl.BlockSpec((None, k, tn), lambda i, j: (layer, 0, joff + j)))
    args.append(w)
    if epi == "qk":
        hg, cos, sin = qk
        in_specs += [
            pl.BlockSpec((1, HEAD_DIM), lambda i, j: (0, 0)),
            pl.BlockSpec((tm, HEAD_DIM), lambda i, j: (i, 0)),
            pl.BlockSpec((tm, HEAD_DIM), lambda i, j: (i, 0)),
        ]
        args += [hg, cos, sin]
    if epi == "residual":
        res, mod, mod_layer, gate_idx = residual
        in_specs += [pl.BlockSpec((tm, tn), lambda i, j: (i, j)),
                     _mod_spec(mod_layer, gate_idx, tm, tn, col_axis=1)]
        args += [res, mod]
    out_shape = [jax.ShapeDtypeStruct((t, n_cols), dt) for dt in out_dtypes]
    out_specs = [pl.BlockSpec((tm, tn), lambda i, j: (i, j)) for _ in out_dtypes]
    scratch = [pltpu.VMEM((tm, k), BF16)] if norm is not None else []
    return pl.pallas_call(
        functools.partial(_linear_kernel, norm=norm is not None, epi=epi, out_scale=out_scale,
                          n_out=len(out_dtypes)),
        out_shape=out_shape,
        grid=(t // tm, n_cols // tn),
        in_specs=in_specs,
        out_specs=out_specs,
        scratch_shapes=scratch,
        compiler_params=_cparams(2),
        name="linear_" + epi,
    )(*args)


def _lambda_value(lam_ref, lam_init):
    l = lam_ref[...]
    s1 = jnp.sum(l[0:1] * l[1:2], axis=-1, keepdims=True)
    s2 = jnp.sum(l[2:3] * l[3:4], axis=-1, keepdims=True)
    return jnp.exp(s1) - jnp.exp(s2) + lam_init


def _nt_dot(a, b):
    return lax.dot_general(a, b, (((1,), (1,)), ((), ())), preferred_element_type=F32)


def _diff_attn_head(q, segs, lam, subg, lam_init):
    state = [None, None]
    for k_ref, v_ref, col0 in segs:
        n_keys = k_ref.shape[0]
        for c0 in range(0, n_keys, KEY_CHUNK):
            rows = slice(c0, min(c0 + KEY_CHUNK, n_keys))
            v = v_ref[rows, col0:col0 + V_DIM]
            for m in range(2):
                cols = slice(col0 + m * HEAD_DIM, col0 + (m + 1) * HEAD_DIM)
                s = _nt_dot(q[:, m * HEAD_DIM:(m + 1) * HEAD_DIM], k_ref[rows, cols])
                smax = jnp.max(s, axis=-1, keepdims=True)
                if state[m] is None:
                    p = jnp.exp2(s - smax)
                    state[m] = (smax, jnp.sum(p, axis=-1, keepdims=True), _dot(p.astype(BF16), v))
                else:
                    mx, den, acc = state[m]
                    new_mx = jnp.maximum(mx, smax)
                    alpha = jnp.exp2(mx - new_mx)
                    p = jnp.exp2(s - new_mx)
                    state[m] = (new_mx, den * alpha + jnp.sum(p, axis=-1, keepdims=True),
                                acc * alpha + _dot(p.astype(BF16), v))
    o = state[0][2] * (1.0 / state[0][1]) - state[1][2] * (lam / state[1][1])
    o = o * lax.rsqrt(jnp.mean(o * o, axis=-1, keepdims=True) + EPS)
    return o * subg * (1.0 - lam_init)


def _ctx_attn_kernel(lam_ref, subg_ref, q_ref, k_ref, v_ref, o_ref, *, lam_init):
    lam = _lambda_value(lam_ref, lam_init)
    subg = subg_ref[...]
    for h in range(N_HEADS):
        sl = slice(h * V_DIM, (h + 1) * V_DIM)
        o = _diff_attn_head(q_ref[:, sl], [(k_ref, v_ref, h * V_DIM)], lam, subg, lam_init)
        o_ref[:, sl] = o.astype(o_ref.dtype)


def _lat_attn_kernel(lam_ref, subg_ref, q_ref, kc_ref, vc_ref, k_ref, v_ref, o_ref, kcb_ref, vcb_ref,
                     *, lam_init):
    @pl.when(pl.program_id(2) == 0)
    def _():
        kcb_ref[...] = kc_ref[...].astype(BF16)
        vcb_ref[...] = vc_ref[...].astype(BF16)

    lam = _lambda_value(lam_ref, lam_init)
    segs = [(kcb_ref, vcb_ref, 0), (k_ref, v_ref, 0)]
    o = _diff_attn_head(q_ref[...], segs, lam, subg_ref[...], lam_init)
    o_ref[...] = o.astype(o_ref.dtype)


def _attention(q, k, v, cache_k, cache_v, lam_vecs, subg, lam_init):
    small = [pl.BlockSpec((N_COND, HEAD_DIM), lambda *_: (0, 0)),
             pl.BlockSpec((1, V_DIM), lambda *_: (0, 0))]
    o_ctx = pl.pallas_call(
        functools.partial(_ctx_attn_kernel, lam_init=lam_init),
        out_shape=jax.ShapeDtypeStruct((N_CTX, N_HEADS * V_DIM), BF16),
        grid=(BATCH,),
        in_specs=small + [pl.BlockSpec((SEQ, N_HEADS * V_DIM), lambda b: (b, 0))] * 3,
        out_specs=pl.BlockSpec((SEQ, N_HEADS * V_DIM), lambda b: (b, 0)),
        compiler_params=_cparams(1),
        name="attn_context",
    )(lam_vecs, subg, q, k, v)
    nq = DEC_SEQ // TQ_LAT
    q0 = N_CTX // TQ_LAT
    kv0 = N_CTX // DEC_SEQ
    o_lat = pl.pallas_call(
        functools.partial(_lat_attn_kernel, lam_init=lam_init),
        out_shape=jax.ShapeDtypeStruct((N_LAT, N_HEADS * V_DIM), BF16),
        grid=(DEC_BATCH, N_HEADS, nq),
        in_specs=small + [
            pl.BlockSpec((TQ_LAT, V_DIM), lambda b, h, i: (q0 + b * nq + i, h)),
            pl.BlockSpec((PAST_LEN, V_DIM), lambda b, h, i: (b, h)),
            pl.BlockSpec((PAST_LEN, V_DIM), lambda b, h, i: (b, h)),
            pl.BlockSpec((DEC_SEQ, V_DIM), lambda b, h, i: (kv0 + b, h)),
            pl.BlockSpec((DEC_SEQ, V_DIM), lambda b, h, i: (kv0 + b, h)),
        ],
        out_specs=pl.BlockSpec((TQ_LAT, V_DIM), lambda b, h, i: (b * nq + i, h)),
        scratch_shapes=[pltpu.VMEM((PAST_LEN, V_DIM), BF16), pltpu.VMEM((PAST_LEN, V_DIM), BF16)],
        compiler_params=_cparams(3),
        name="attn_latent",
    )(lam_vecs, subg, q, cache_k, cache_v, k, v)
    return jnp.concatenate([o_ctx, o_lat], axis=0)


def _pool_kernel(x_ref, xp_ref, xn_ref, g_ref, shift_ref, scale_ref, gate_ref, w_ref, ps_ref, o_ref):
    i = pl.program_id(0)
    tm = x_ref.shape[0]
    row0 = i * tm
    seq_pos = jnp.where(row0 < N_CTX, row0 % SEQ, (row0 - N_CTX) % DEC_SEQ)
    seq_len = jnp.where(row0 < N_CTX, SEQ, DEC_SEQ)
    has_prev = seq_pos > 0
    has_next = seq_pos + tm < seq_len
    g, shift, scale = g_ref[...], shift_ref[...], scale_ref[...]
    x = x_ref[...]
    h = _rms_mod(x, g, shift, scale)
    halo = xp_ref.shape[0]
    hp = _rms_mod(xp_ref[...], g, shift, scale) * has_prev.astype(F32)
    hn = _rms_mod(xn_ref[...], g, shift, scale) * has_next.astype(F32)
    t_c = lax.broadcasted_iota(jnp.int32, (tm, tm), 0)
    s_c = lax.broadcasted_iota(jnp.int32, (tm, tm), 1)
    t_h = lax.broadcasted_iota(jnp.int32, (tm, halo), 0)
    s_h = lax.broadcasted_iota(jnp.int32, (tm, halo), 1)
    t_1 = lax.broadcasted_iota(jnp.int32, (tm, 1), 0)
    for grp, win in enumerate(POOL_WINDOWS):
        half = win // 2
        sl = slice(grp * POOL_GROUP, (grp + 1) * POOL_GROUP)
        band_c = ((s_c >= t_c - half) & (s_c < t_c + half)).astype(BF16)
        band_p = (s_h - halo >= t_h - half).astype(BF16)
        band_n = (s_h + tm < t_h + half).astype(BF16)
        tot = None
        for band, src in ((band_c, h[:, sl]), (band_p, hp[:, sl]), (band_n, hn[:, sl])):
            hi, lo = _split_bf16(src)
            part = _dot(band, hi) + _dot(band, lo)
            tot = part if tot is None else tot + part
        lo_edge = jnp.where(has_prev, t_1 - half, jnp.maximum(t_1 - half, 0))
        hi_edge = jnp.where(has_next, t_1 + half, jnp.minimum(t_1 + half, tm))
        diff = tot / (hi_edge - lo_edge).astype(F32) - h[:, sl]
        y = _dot(diff.astype(BF16), w_ref[grp]) * ps_ref[:, sl]
        o_ref[:, sl] = x[:, sl] + gate_ref[:, sl] * y


def _pool_mixer(x, g, mod, layer, pool_w, j, pool_scale, tm=TM_POOL, halo=8):
    t, d = x.shape
    nb = tm // halo
    last = t // halo - 1
    return pl.pallas_call(
        _pool_kernel,
        out_shape=jax.ShapeDtypeStruct((t, d), F32),
        grid=(t // tm,),
        in_specs=[
            pl.BlockSpec((tm, d), lambda i: (i, 0)),
            pl.BlockSpec((halo, d), lambda i: (jnp.maximum(i * nb - 1, 0), 0)),
            pl.BlockSpec((halo, d), lambda i: (jnp.minimum((i + 1) * nb, last), 0)),
            pl.BlockSpec((1, d), lambda i: (0, 0)),
            _mod_spec(layer, 0, tm, d), _mod_spec(layer, 1, tm, d), _mod_spec(layer, 2, tm, d),
            pl.BlockSpec((None, len(POOL_WINDOWS), POOL_GROUP, POOL_GROUP), lambda i: (j, 0, 0, 0)),
            pl.BlockSpec((1, d), lambda i: (0, 0)),
        ],
        out_specs=pl.BlockSpec((tm, d), lambda i: (i, 0)),
        compiler_params=_cparams(1),
        name="pool_mixer",
    )(x, x, x, g, mod, mod, mod, pool_w, pool_scale)


def _gmlp_out_kernel(u_ref, v_ref, lg_ref, lb_ref, ws_ref, bs_ref, w_ref, res_ref, gate_ref, o_ref, a_ref):
    @pl.when(pl.program_id(1) == 0)
    def _():
        v = v_ref[...].astype(F32)
        mu = jnp.mean(v, axis=-1, keepdims=True)
        vc = v - mu
        vn = vc * lax.rsqrt(jnp.mean(vc * vc, axis=-1, keepdims=True) + EPS) * lg_ref[...] + lb_ref[...]
        vn = vn.astype(BF16)
        gw = GMLP_WIDTH // GMLP_GROUPS
        for c in range(v.shape[0] // CHUNK):
            rows = slice(c * CHUNK, (c + 1) * CHUNK)
            for grp in range(GMLP_GROUPS):
                cols = slice(grp * gw, (grp + 1) * gw)
                mixed = _dot(ws_ref[grp], vn[rows, cols]) + bs_ref[:, grp:grp + 1]
                a_ref[rows, cols] = (u_ref[rows, cols].astype(F32) * mixed).astype(BF16)

    acc = _dot(a_ref[...], w_ref[...])
    o_ref[...] = res_ref[...] + gate_ref[...] * acc


def _gmlp_out(z, ln_g, ln_b, w_s, b_s_t, w_out, j, res, mod, layer, gate_idx, tm=TM_GMLP, tn=TN_LIN):
    t = z.shape[0]
    d = w_out.shape[-1]
    return pl.pallas_call(
        _gmlp_out_kernel,
        out_shape=jax.ShapeDtypeStruct((t, d), F32),
        grid=(t // tm, d // tn),
        in_specs=[
            pl.BlockSpec((tm, GMLP_WIDTH), lambda i, n: (i, 0)),
            pl.BlockSpec((tm, GMLP_WIDTH), lambda i, n: (i, 1)),
            pl.BlockSpec((1, GMLP_WIDTH), lambda i, n: (0, 0)),
            pl.BlockSpec((1, GMLP_WIDTH), lambda i, n: (0, 0)),
            pl.BlockSpec((None, GMLP_GROUPS, CHUNK, CHUNK), lambda i, n: (j, 0, 0, 0)),
            pl.BlockSpec((CHUNK, GMLP_GROUPS), lambda i, n: (0, 0)),
            pl.BlockSpec((None, GMLP_WIDTH, tn), lambda i, n: (j, 0, n)),
            pl.BlockSpec((tm, tn), lambda i, n: (i, n)),
            _mod_spec(layer, gate_idx, tm, tn, col_axis=1),
        ],
        out_specs=pl.BlockSpec((tm, tn), lambda i, n: (i, n)),
        scratch_shapes=[pltpu.VMEM((tm, GMLP_WIDTH), BF16)],
        compiler_params=_cparams(2),
        name="gmlp_out",
    )(z, z, ln_g, ln_b, w_s, b_s_t, w_out, res, mod)


def _router_kernel(x_ref, g_ref, shift_ref, scale_ref, w_ref, b_ref, idx_ref, gates_ref, rank_ref, cnt_ref,
                   run_ref):
    tm = x_ref.shape[0]

    @pl.when(pl.program_id(0) == 0)
    def _():
        run_ref[...] = jnp.zeros(run_ref.shape, F32)

    h = _rms_mod(x_ref[...], g_ref[...], shift_ref[...], scale_ref[...])
    logits = _dot3(h, w_ref[...]) + b_ref[...]
    lane = lax.broadcasted_iota(jnp.int32, logits.shape, 1).astype(F32)
    vals, idxs = [], []
    for _ in range(TOP_K):
        m = jnp.max(logits, axis=-1, keepdims=True)
        am = jnp.min(jnp.where(logits == m, lane, float(N_EXPERTS)), axis=-1, keepdims=True)
        vals.append(m)
        idxs.append(am)
        logits = jnp.where(lane == am, -jnp.inf, logits)
    exps = [jnp.exp(v - vals[0]) for v in vals]
    den = functools.reduce(jnp.add, exps)
    wide = lax.broadcasted_iota(jnp.int32, (tm, LANES), 1).astype(F32)
    chosen = functools.reduce(jnp.add, [(wide == e).astype(F32) for e in idxs])
    row = lax.broadcasted_iota(jnp.int32, (tm, tm), 0)
    col = lax.broadcasted_iota(jnp.int32, (tm, tm), 1)
    before = _dot((col < row).astype(BF16), chosen.astype(BF16)) + run_ref[...]
    run_ref[...] = run_ref[...] + jnp.sum(chosen, axis=0, keepdims=True)
    out_lane = lax.broadcasted_iota(jnp.int32, idx_ref.shape, 1)
    idx_out = jnp.zeros(idx_ref.shape, F32)
    gate_out = jnp.zeros(gates_ref.shape, F32)
    rank_out = jnp.zeros(rank_ref.shape, F32)
    for r in range(TOP_K):
        idx_out = jnp.where(out_lane == r, idxs[r], idx_out)
        gate_out = jnp.where(out_lane == r, exps[r] / den, gate_out)
        rank_r = jnp.sum(jnp.where(wide == idxs[r], before, 0.0), axis=-1, keepdims=True)
        rank_out = jnp.where(out_lane == r, rank_r, rank_out)
    idx_ref[...] = idx_out.astype(jnp.int32)
    gates_ref[...] = gate_out
    rank_ref[...] = rank_out.astype(jnp.int32)
    cnt_ref[...] = jnp.broadcast_to(run_ref[...], cnt_ref.shape).astype(jnp.int32)


def _router(x, g, mod, layer, w_router, b_router, tm=TM_TOK):
    t, d = x.shape
    tok_out = lambda: pl.BlockSpec((tm, LANES), lambda i: (i, 0))
    return pl.pallas_call(
        _router_kernel,
        out_shape=[jax.ShapeDtypeStruct((t, LANES), jnp.int32),
                   jax.ShapeDtypeStruct((t, LANES), F32),
                   jax.ShapeDtypeStruct((t, LANES), jnp.int32),
                   jax.ShapeDtypeStruct((N_COND, LANES), jnp.int32)],
        grid=(t // tm,),
        in_specs=[
            pl.BlockSpec((tm, d), lambda i: (i, 0)),
            pl.BlockSpec((1, d), lambda i: (0, 0)),
            _mod_spec(layer, 3, tm, d), _mod_spec(layer, 4, tm, d),
            pl.BlockSpec((None, d, N_EXPERTS), lambda i: (layer, 0, 0)),
            pl.BlockSpec((None, 1, N_EXPERTS), lambda i: (layer, 0, 0)),
        ],
        out_specs=[tok_out(), tok_out(), tok_out(), pl.BlockSpec((N_COND, LANES), lambda i: (0, 0))],
        scratch_shapes=[pltpu.VMEM((1, LANES), F32)],
        compiler_params=_cparams(1),
        name="moe_router",
    )(x, g, mod, mod, w_router, b_router)


def _dispatch_kernel(dest_ref, pad_ref, x_ref, g_ref, shift_ref, scale_ref, xs_ref, h_ref, z_ref, sem, pad_sem):
    i = pl.program_id(0)
    tm = x_ref.shape[0]

    @pl.when(i == 0)
    def _():
        z_ref[...] = jnp.zeros(z_ref.shape, F32)

    h = _rms_mod(x_ref[...], g_ref[...], shift_ref[...], scale_ref[...])
    h_ref[...] = h.reshape(h_ref.shape)
    base = i * (tm * TOP_K)

    def issue(r, carry):
        for kk in range(TOP_K):
            pltpu.make_async_copy(h_ref.at[r], xs_ref.at[dest_ref[base + r * TOP_K + kk]], sem).start()
        return carry

    lax.fori_loop(0, tm, issue, 0)

    def issue_pad(c, carry):
        pltpu.make_async_copy(z_ref.at[c], xs_ref.at[pad_ref[i * PADS_PER_STEP + c]], pad_sem).start()
        return carry

    lax.fori_loop(0, PADS_PER_STEP, issue_pad, 0)
    for _ in range(TOP_K):
        pltpu.make_async_copy(h_ref, xs_ref.at[pl.ds(0, tm)], sem).wait()
    pltpu.make_async_copy(z_ref, xs_ref.at[pl.ds(0, PADS_PER_STEP)], pad_sem).wait()


def _dispatch(x, g, mod, layer, dest, pad_rows, tm=TM_TOK):
    t, d = x.shape
    return pl.pallas_call(
        _dispatch_kernel,
        out_shape=jax.ShapeDtypeStruct((N_MOE_ROWS, SLABS, LANES), F32),
        grid_spec=pltpu.PrefetchScalarGridSpec(
            num_scalar_prefetch=2,
            grid=(t // tm,),
            in_specs=[
                pl.BlockSpec((tm, d), lambda i, *_: (i, 0)),
                pl.BlockSpec((1, d), lambda i, *_: (0, 0)),
                _mod_spec(layer, 3, tm, d), _mod_spec(layer, 4, tm, d),
            ],
            out_specs=pl.BlockSpec(memory_space=pl.ANY),
            scratch_shapes=[pltpu.VMEM((tm, SLABS, LANES), F32), pltpu.VMEM((PADS_PER_STEP, SLABS, LANES), F32),
                            pltpu.SemaphoreType.DMA, pltpu.SemaphoreType.DMA],
        ),
        compiler_params=_cparams(1),
        name="moe_dispatch",
    )(dest, pad_rows, x, g, mod, mod)


def _expert_changed(te_ref, i):
    return (i == 0) | (te_ref[i] != te_ref[jnp.maximum(i - 1, 0)])


def _expert_up_kernel(te_ref, nu_ref, x_ref, wg_ref, wu_ref, bg_ref, bu_ref, o_ref, wgb_ref, wub_ref):
    i = pl.program_id(1)

    @pl.when(_expert_changed(te_ref, i))
    def _():
        wgb_ref[...] = wg_ref[...].astype(BF16)
        wub_ref[...] = wu_ref[...].astype(BF16)

    @pl.when(i < nu_ref[0])
    def _():
        x = x_ref[...].reshape(x_ref.shape[0], D_MODEL).astype(BF16)
        gate = jnp.minimum(_dot(x, wgb_ref[...]) + bg_ref[...], SWIGLU_LIMIT)
        up = jnp.clip(_dot(x, wub_ref[...]) + bu_ref[...], -SWIGLU_LIMIT, SWIGLU_LIMIT)
        act = gate * (1.0 / (1.0 + jnp.exp(-SWIGLU_ALPHA * gate))) * (up + 1.0)
        o_ref[...] = act.astype(o_ref.dtype)

    @pl.when(i >= nu_ref[0])
    def _():
        o_ref[...] = jnp.zeros(o_ref.shape, o_ref.dtype)


def _expert_down_kernel(te_ref, nu_ref, a_ref, w_ref, b_ref, o_ref, wb_ref):
    i = pl.program_id(0)

    @pl.when(_expert_changed(te_ref, i))
    def _():
        wb_ref[...] = w_ref[...].astype(BF16)

    @pl.when(i < nu_ref[0])
    def _():
        y = _dot(a_ref[...], wb_ref[...]) + b_ref[...]
        o_ref[...] = y.reshape(o_ref.shape)

    @pl.when(i >= nu_ref[0])
    def _():
        o_ref[...] = jnp.zeros(o_ref.shape, o_ref.dtype)


def _experts(xs, tile_expert, n_used, layer, w_gu, b_gu, w_down, b_down, tm=TM_MOE, th=TH_MOE):
    d = D_MODEL
    n_j = D_EXPERT // th
    row = lambda i, nu: jnp.minimum(i, jnp.maximum(nu[0] - 1, 0))
    act = pl.pallas_call(
        _expert_up_kernel,
        out_shape=jax.ShapeDtypeStruct((N_MOE_ROWS, D_EXPERT), BF16),
        grid_spec=pltpu.PrefetchScalarGridSpec(
            num_scalar_prefetch=2,
            grid=(n_j, N_MOE_TILES),
            in_specs=[
                pl.BlockSpec((tm, SLABS, LANES), lambda j, i, te, nu: (row(i, nu), 0, 0)),
                pl.BlockSpec((None, None, d, th), lambda j, i, te, nu: (layer, te[i], 0, j)),
                pl.BlockSpec((None, None, d, th), lambda j, i, te, nu: (layer, te[i], 0, n_j + j)),
                pl.BlockSpec((None, None, 1, th), lambda j, i, te, nu: (layer, te[i], 0, j)),
                pl.BlockSpec((None, None, 1, th), lambda j, i, te, nu: (layer, te[i], 0, n_j + j)),
            ],
            out_specs=pl.BlockSpec((tm, th), lambda j, i, te, nu: (i, j)),
            scratch_shapes=[pltpu.VMEM((d, th), BF16), pltpu.VMEM((d, th), BF16)],
        ),
        compiler_params=_cparams(2),
        name="moe_expert_up",
    )(tile_expert, n_used, xs, w_gu, w_gu, b_gu, b_gu)
    return pl.pallas_call(
        _expert_down_kernel,
        out_shape=jax.ShapeDtypeStruct((N_MOE_ROWS, SLABS, LANES), F32),
        grid_spec=pltpu.PrefetchScalarGridSpec(
            num_scalar_prefetch=2,
            grid=(N_MOE_TILES,),
            in_specs=[
                pl.BlockSpec((tm, D_EXPERT), lambda i, te, nu: (row(i, nu), 0)),
                pl.BlockSpec((None, None, D_EXPERT, d), lambda i, te, nu: (layer, te[i], 0, 0)),
                pl.BlockSpec((None, None, 1, d), lambda i, te, nu: (layer, te[i], 0, 0)),
            ],
            out_specs=pl.BlockSpec((tm, SLABS, LANES), lambda i, te, nu: (i, 0, 0)),
            scratch_shapes=[pltpu.VMEM((D_EXPERT, d), BF16)],
        ),
        compiler_params=_cparams(1),
        name="moe_expert_down",
    )(tile_expert, n_used, act, w_down, b_down)


def _combine_kernel(dest_ref, x_ref, gates_ref, gate_ref, y_ref, o_ref, buf_ref, sem):
    i = pl.program_id(0)
    n = pl.num_programs(0)
    tm = x_ref.shape[0]

    def issue(tile, slot):
        base = tile * (tm * TOP_K)

        def body(r, carry):
            for kk in range(TOP_K):
                pltpu.make_async_copy(y_ref.at[dest_ref[base + r * TOP_K + kk]],
                                      buf_ref.at[slot, kk * tm + r], sem.at[slot]).start()
            return carry

        lax.fori_loop(0, tm, body, 0)

    @pl.when(i == 0)
    def _():
        issue(0, 0)

    @pl.when(i + 1 < n)
    def _():
        issue(i + 1, (i + 1) % 2)

    slot = i % 2
    pltpu.make_async_copy(y_ref.at[pl.ds(0, TOP_K * tm)], buf_ref.at[slot], sem.at[slot]).wait()
    gates = gates_ref[...]
    acc = None
    for kk in range(TOP_K):
        part = gates[:, kk:kk + 1] * buf_ref[slot, pl.ds(kk * tm, tm)].reshape(tm, D_MODEL)
        acc = part if acc is None else acc + part
    o_ref[...] = x_ref[...] + gate_ref[...] * acc


def _combine(x, y, gates, dest, mod, layer, tm=TM_TOK):
    t, d = x.shape
    return pl.pallas_call(
        _combine_kernel,
        out_shape=jax.ShapeDtypeStruct((t, d), F32),
        grid_spec=pltpu.PrefetchScalarGridSpec(
            num_scalar_prefetch=1,
            grid=(t // tm,),
            in_specs=[
                pl.BlockSpec((tm, d), lambda i, *_: (i, 0)),
                pl.BlockSpec((tm, LANES), lambda i, *_: (i, 0)),
                _mod_spec(layer, 5, tm, d),
                pl.BlockSpec(memory_space=pl.ANY),
            ],
            out_specs=pl.BlockSpec((tm, d), lambda i, *_: (i, 0)),
            scratch_shapes=[pltpu.VMEM((2, TOP_K * tm, SLABS, LANES), F32), pltpu.SemaphoreType.DMA((2,))],
        ),
        compiler_params=_cparams(1),
        name="moe_combine",
    )(dest, x, gates, mod, y)


def _dispatch_plan(idx, rank, counts, tm=TM_MOE):
    padded = (counts + tm - 1) // tm * tm
    pend = jnp.cumsum(padded)
    pstart = pend - padded
    experts = jnp.arange(N_EXPERTS, dtype=jnp.int32)
    flat_e = idx.reshape(-1)
    dest = rank.reshape(-1) + jnp.sum(jnp.where(flat_e[:, None] == experts[None, :], pstart[None, :], 0), axis=1)
    tile_start = jnp.arange(N_MOE_TILES, dtype=jnp.int32) * tm
    tile_expert = jnp.minimum(jnp.sum((pend[None, :] <= tile_start[:, None]).astype(jnp.int32), axis=1),
                              N_EXPERTS - 1)
    n_used = (pend[-1] // tm).astype(jnp.int32).reshape(1)
    seg_len = jnp.concatenate([padded - counts, (N_MOE_ROWS - pend[-1])[None]])
    seg_start = jnp.concatenate([pstart + counts, pend[-1:]])
    seg_end = jnp.cumsum(seg_len)
    seg_lo = seg_end - seg_len
    slot = jnp.arange(N_PAD_SLOTS, dtype=jnp.int32)[:, None]
    in_seg = (slot >= seg_lo[None, :]) & (slot < seg_end[None, :])
    pad_rows = slot[:, 0] + jnp.sum(jnp.where(in_seg, (seg_start - seg_lo)[None, :], 0), axis=1)
    return dest.astype(jnp.int32), pad_rows.astype(jnp.int32), tile_expert.astype(jnp.int32), n_used


def _moe_block(x, g, mod, layer, w_router, b_router, w_gu, b_gu, w_down, b_down):
    idx, gates, rank, cnt = _router(x, g, mod, layer, w_router, b_router)
    dest, pad_rows, tile_expert, n_used = _dispatch_plan(idx[:, :TOP_K], rank[:, :TOP_K], cnt[0, :N_EXPERTS])
    xs = _dispatch(x, g, mod, layer, dest, pad_rows)
    y = _experts(xs, tile_expert, n_used, layer, w_gu, b_gu, w_down, b_down)
    return _combine(x, y, gates, dest, mod, layer)


def _rope_tables():
    t = jnp.arange(DEC_SEQ)
    row = (t // GRID_W).astype(F32)
    col = (t % GRID_W).astype(F32)
    n_freq = HEAD_DIM // 4
    inv = ROPE_THETA ** (-jnp.arange(n_freq, dtype=F32) / n_freq)
    ang = jnp.concatenate([row[:, None] * inv] * 2 + [col[:, None] * inv] * 2, axis=1)
    cos = jnp.concatenate([jnp.ones((N_CTX, HEAD_DIM), F32)] + [jnp.cos(ang)] * DEC_BATCH, axis=0)
    sin = jnp.concatenate([jnp.zeros((N_CTX, HEAD_DIM), F32)] + [jnp.sin(ang)] * DEC_BATCH, axis=0)
    return cos, sin


def kernel(x_prompt, x_sample, c, cache_k, cache_v, c_ctx, w_mod, b_mod, norm1_g, norm2_g, attn_w_qkv, attn_q_norm, attn_k_norm, attn_lambda_q1, attn_lambda_k1, attn_lambda_q2, attn_lambda_k2, attn_subln_g, attn_w_o, pool_w, pool_scale, gmlp_w_in, gmlp_ln_g, gmlp_ln_b, gmlp_w_s, gmlp_b_s, gmlp_w_out, moe_w_router, moe_b_router, moe_w_gu, moe_b_gu, moe_w_down, moe_b_down):
    d = D_MODEL
    x = jnp.concatenate([x_prompt.reshape(N_CTX, d), x_sample.reshape(N_LAT, d)], axis=0)
    cond = jnp.concatenate([c_ctx[None, :], c, jnp.zeros((N_COND - 1 - DEC_BATCH, d), F32)], axis=0)
    mod = _modulation(cond, w_mod, b_mod).reshape(DEPTH, N_COND, 6, 1, d)
    cos, sin = _rope_tables()
    w_qkv, w_o, w_pool = attn_w_qkv.astype(BF16), attn_w_o.astype(BF16), pool_w.astype(BF16)
    w_in, w_s, w_out = gmlp_w_in.astype(BF16), gmlp_w_s.astype(BF16), gmlp_w_out.astype(BF16)
    b_router = moe_b_router.reshape(DEPTH, 1, N_EXPERTS)
    b_gu = moe_b_gu.reshape(DEPTH, N_EXPERTS, 1, 2 * D_EXPERT)
    b_down = moe_b_down.reshape(DEPTH, N_EXPERTS, 1, d)
    ctx_k, ctx_v = [], []
    for layer in range(DEPTH):
        kind, j = layer % 3, layer // 3
        g1 = norm1_g[layer].reshape(1, d)
        norm1 = (g1, mod, layer, 0, 1)
        if kind == 0:
            lam_init = 0.8 - 0.6 * math.exp(-0.3 * layer)
            (q,) = _linear(x, w_qkv, j, 0, QK_COLS, [BF16], norm=norm1, epi="qk",
                           qk=(attn_q_norm[j].reshape(1, HEAD_DIM), cos, sin),
                           out_scale=HEAD_DIM ** -0.5 * math.log2(math.e))
            k32, k16 = _linear(x, w_qkv, j, QK_COLS, QK_COLS, [F32, BF16], norm=norm1, epi="qk",
                               qk=(attn_k_norm[j].reshape(1, HEAD_DIM), cos, sin))
            v32, v16 = _linear(x, w_qkv, j, 2 * QK_COLS, N_HEADS * V_DIM, [F32, BF16], norm=norm1)
            ctx_k.append(k32[:N_CTX].reshape(BATCH, SEQ, N_HEADS, 2, HEAD_DIM))
            ctx_v.append(v32[:N_CTX].reshape(BATCH, SEQ, N_HEADS, V_DIM))
            lam_vecs = jnp.concatenate(
                [attn_lambda_q1[j][None], attn_lambda_k1[j][None], attn_lambda_q2[j][None],
                 attn_lambda_k2[j][None], jnp.zeros((N_COND - 4, HEAD_DIM), F32)], axis=0)
            o = _attention(q, k16, v16,
                           cache_k[:, j].reshape(DEC_BATCH * PAST_LEN, QK_COLS),
                           cache_v[:, j].reshape(DEC_BATCH * PAST_LEN, N_HEADS * V_DIM),
                           lam_vecs, attn_subln_g[j].reshape(1, V_DIM), lam_init)
            (x,) = _linear(o, w_o, j, 0, d, [F32], epi="residual", residual=(x, mod, layer, 2))
        elif kind == 1:
            x = _pool_mixer(x, g1, mod, layer, w_pool, j, pool_scale[j].reshape(1, d))
        else:
            (z,) = _linear(x, w_in, j, 0, 2 * GMLP_WIDTH, [BF16], norm=norm1, epi="gelu")
            x = _gmlp_out(z, gmlp_ln_g[j].reshape(1, -1), gmlp_ln_b[j].reshape(1, -1), w_s,
                          gmlp_b_s[j].T, w_out, j, x, mod, layer, 2)
        x = _moe_block(x, norm2_g[layer].reshape(1, d), mod, layer, moe_w_router, b_router,
                       moe_w_gu, b_gu, moe_w_down, b_down)
    y_prompt = x[:N_CTX].reshape(BATCH, SEQ, d)
    y_sample = x[N_CTX:].reshape(DEC_BATCH, DEC_SEQ, d)
    return (y_prompt, y_sample, jnp.stack(ctx_k, axis=1), jnp.stack(ctx_v, axis=1))
```

```python
import functools
import math

import jax
import jax.numpy as jnp
from jax import lax
from jax.experimental import pallas as pl
from jax.experimental.pallas import tpu as pltpu

F32 = jnp.float32
BF16 = jnp.bfloat16

D_MODEL = 2048
BATCH = 32
SEQ = 256
DEPTH = 4
DEC_BATCH = 2
DEC_SEQ = 4096
PAST_LEN = 256
GRID_W = 64
EPS = 1e-6
N_HEADS = 8
HEAD_DIM = 128
V_DIM = 2 * HEAD_DIM
ROPE_THETA = 10000.0
POOL_WINDOWS = (2, 4, 8, 16)
POOL_GROUP = D_MODEL // 4
CHUNK = 128
GMLP_WIDTH = 2 * D_MODEL
GMLP_GROUPS = 8
N_EXPERTS = 32
TOP_K = 4
D_EXPERT = D_MODEL
SWIGLU_LIMIT = 7.0
SWIGLU_ALPHA = 1.702

N_CTX = BATCH * SEQ
N_LAT = DEC_BATCH * DEC_SEQ
N_TOK = N_CTX + N_LAT
N_COND = 8
QK_COLS = 2 * N_HEADS * HEAD_DIM

VMEM_LIMIT = 56 * 1024 * 1024
LANES = 128
SLABS = D_MODEL // LANES

TM_LIN = 512
TN_LIN = 1024
TQ_LAT = 256
KEY_CHUNK = 512
TM_POOL = 256
TM_GMLP = 256
TM_MOE = 256
TH_MOE = 1024
TM_TOK = 256
N_ASSIGN = N_TOK * TOP_K
N_MOE_TILES = N_ASSIGN // TM_MOE + N_EXPERTS
N_MOE_ROWS = N_MOE_TILES * TM_MOE
N_PAD_SLOTS = N_EXPERTS * TM_MOE
PADS_PER_STEP = N_PAD_SLOTS // (N_TOK // TM_TOK)


def _cparams(n_axes):
    return pltpu.CompilerParams(
        dimension_semantics=("arbitrary",) * n_axes, vmem_limit_bytes=VMEM_LIMIT)


def _cond_of_row_tile(i, tm):
    row = i * tm
    return jnp.where(row < N_CTX, 0, 1 + (row - N_CTX) // DEC_SEQ)


def _mod_spec(layer, idx, tm, width, col_axis=None):
    def index_map(*ids):
        i = ids[0]
        jcol = ids[col_axis] if col_axis is not None else 0
        return (layer, _cond_of_row_tile(i, tm), idx, 0, jcol)
    return pl.BlockSpec((None, None, None, 1, width), index_map)


def _rms_mod(x, g, shift, scale):
    xf = x.astype(F32)
    y = xf * lax.rsqrt(jnp.mean(xf * xf, axis=-1, keepdims=True) + EPS)
    return (y * g) * (1.0 + scale) + shift


def _split_bf16(x):
    hi = x.astype(BF16)
    lo = (x - hi.astype(F32)).astype(BF16)
    return hi, lo


def _dot(a, b):
    return jnp.dot(a, b, preferred_element_type=F32)


def _dot3(a, b):
    a_hi, a_lo = _split_bf16(a)
    b_hi, b_lo = _split_bf16(b)
    return _dot(a_hi, b_hi) + (_dot(a_lo, b_hi) + _dot(a_hi, b_lo))


def _modulation_kernel(cond_ref, w_ref, b_ref, o_ref):
    cnd = cond_ref[...]
    act = cnd * (1.0 / (1.0 + jnp.exp(-cnd)))
    o_ref[...] = _dot3(act, w_ref[...]) + b_ref[...]


def _modulation(cond, w_mod, b_mod, tn=1024):
    n_out = w_mod.shape[-1]
    return pl.pallas_call(
        _modulation_kernel,
        out_shape=jax.ShapeDtypeStruct((DEPTH, N_COND, n_out), F32),
        grid=(DEPTH, n_out // tn),
        in_specs=[
            pl.BlockSpec((N_COND, D_MODEL), lambda l, j: (0, 0)),
            pl.BlockSpec((None, D_MODEL, tn), lambda l, j: (l, 0, j)),
            pl.BlockSpec((None, 1, tn), lambda l, j: (l, 0, j)),
        ],
        out_specs=pl.BlockSpec((None, N_COND, tn), lambda l, j: (l, 0, j)),
        compiler_params=_cparams(2),
        name="modulation",
    )(cond, w_mod, b_mod.reshape(DEPTH, 1, n_out))


def _rope_rotate(x):
    lane = lax.broadcasted_iota(jnp.int32, x.shape, 1)
    first_half = (lane % (HEAD_DIM // 2)) < (HEAD_DIM // 4)
    return jnp.where(first_half, -pltpu.roll(x, HEAD_DIM - HEAD_DIM // 4, 1),
                     pltpu.roll(x, HEAD_DIM // 4, 1))


def _linear_kernel(*refs, norm, epi, out_scale, n_out):
    it = iter(refs)
    x_ref = next(it)
    if norm:
        g_ref, shift_ref, scale_ref = next(it), next(it), next(it)
    w_ref = next(it)
    if epi == "qk":
        hg_ref, cos_ref, sin_ref = next(it), next(it), next(it)
    if epi == "residual":
        res_ref, gate_ref = next(it), next(it)
    out_refs = [next(it) for _ in range(n_out)]
    if norm:
        h_ref = next(it)

        @pl.when(pl.program_id(1) == 0)
        def _():
            h_ref[...] = _rms_mod(x_ref[...], g_ref[...], shift_ref[...], scale_ref[...]).astype(BF16)

        a = h_ref[...]
    else:
        a = x_ref[...]
    acc = _dot(a, w_ref[...])
    if epi == "qk":
        cos, sin, hg = cos_ref[...], sin_ref[...], hg_ref[...]
        for c in range(acc.shape[1] // HEAD_DIM):
            sl = slice(c * HEAD_DIM, (c + 1) * HEAD_DIM)
            p = acc[:, sl]
            qn = p * lax.rsqrt(jnp.mean(p * p, axis=-1, keepdims=True) + EPS) * hg
            r = (qn * cos + _rope_rotate(qn) * sin) * out_scale
            for o_ref in out_refs:
                o_ref[:, sl] = r.astype(o_ref.dtype)
        return
    if epi == "gelu":
        acc = jax.nn.gelu(acc)
    elif epi == "residual":
        acc = res_ref[...] + gate_ref[...] * acc
    for o_ref in out_refs:
        o_ref[...] = acc.astype(o_ref.dtype)


def _linear(x, w, layer, col_off, n_cols, out_dtypes, *, norm=None, epi="plain", qk=None,
            residual=None, out_scale=1.0, tm=TM_LIN, tn=TN_LIN):
    t, k = x.shape
    joff = col_off // tn
    in_specs = [pl.BlockSpec((tm, k), lambda i, j: (i, 0))]
    args = [x]
    if norm is not None:
        g, mod, mod_layer, shift_idx, scale_idx = norm
        in_specs += [pl.BlockSpec((1, k), lambda i, j: (0, 0)),
                     _mod_spec(mod_layer, shift_idx, tm, k), _mod_spec(mod_layer, scale_idx, tm, k)]
        args += [g, mod, mod]
    in_specs.append(pl.BlockSpec((None, k, tn), lambda i, j: (layer, 0, joff + j)))
    args.append(w)
    if epi == "qk":
        hg, cos, sin = qk
        in_specs += [
            pl.BlockSpec((1, HEAD_DIM), lambda i, j: (0, 0)),
            pl.BlockSpec((tm, HEAD_DIM), lambda i, j: (i, 0)),
            pl.BlockSpec((tm, HEAD_DIM), lambda i, j: (i, 0)),
        ]
        args += [hg, cos, sin]
    if epi == "residual":
        res, mod, mod_layer, gate_idx = residual
        in_specs += [pl.BlockSpec((tm, tn), lambda i, j: (i, j)),
                     _mod_spec(mod_layer, gate_idx, tm, tn, col_axis=1)]
        args += [res, mod]
    out_shape = [jax.ShapeDtypeStruct((t, n_cols), dt) for dt in out_dtypes]
    out_specs = [pl.BlockSpec((tm, tn), lambda i, j: (i, j)) for _ in out_dtypes]
    scratch = [pltpu.VMEM((tm, k), BF16)] if norm is not None else []
    return pl.pallas_call(
        functools.partial(_linear_kernel, norm=norm is not None, epi=epi, out_scale=out_scale,
                          n_out=len(out_dtypes)),
        out_shape=out_shape,
        grid=(t // tm, n_cols // tn),
        in_specs=in_specs,
        out_specs=out_specs,
        scratch_shapes=scratch,
        compiler_params=_cparams(2),
        name="linear_" + epi,
    )(*args)


def _lambda_value(lam_ref, lam_init):
    l = lam_ref[...]
    s1 = jnp.sum(l[0:1] * l[1:2], axis=-1, keepdims=True)
    s2 = jnp.sum(l[2:3] * l[3:4], axis=-1, keepdims=True)
    return jnp.exp(s1) - jnp.exp(s2) + lam_init


def _nt_dot(a, b):
    return lax.dot_general(a, b, (((1,), (1,)), ((), ())), preferred_element_type=F32)


def _diff_attn_head(q, segs, lam, subg, lam_init):
    state = [None, None]
    for k_ref, v_ref, col0 in segs:
        n_keys = k_ref.shape[0]
        for c0 in range(0, n_keys, KEY_CHUNK):
            rows = slice(c0, min(c0 + KEY_CHUNK, n_keys))
            v = v_ref[rows, col0:col0 + V_DIM]
            for m in range(2):
                cols = slice(col0 + m * HEAD_DIM, col0 + (m + 1) * HEAD_DIM)
                s = _nt_dot(q[:, m * HEAD_DIM:(m + 1) * HEAD_DIM], k_ref[rows, cols])
                smax = jnp.max(s, axis=-1, keepdims=True)
                if state[m] is None:
                    p = jnp.exp2(s - smax)
                    state[m] = (smax, jnp.sum(p, axis=-1, keepdims=True), _dot(p.astype(BF16), v))
                else:
                    mx, den, acc = state[m]
                    new_mx = jnp.maximum(mx, smax)
                    alpha = jnp.exp2(mx - new_mx)
                    p = jnp.exp2(s - new_mx)
                    state[m] = (new_mx, den * alpha + jnp.sum(p, axis=-1, keepdims=True),
                                acc * alpha + _dot(p.astype(BF16), v))
    o = state[0][2] * (1.0 / state[0][1]) - state[1][2] * (lam / state[1][1])
    o = o * lax.rsqrt(jnp.mean(o * o, axis=-1, keepdims=True) + EPS)
    return o * subg * (1.0 - lam_init)


def _ctx_attn_kernel(lam_ref, subg_ref, q_ref, k_ref, v_ref, o_ref, *, lam_init):
    lam = _lambda_value(lam_ref, lam_init)
    subg = subg_ref[...]
    for h in range(N_HEADS):
        sl = slice(h * V_DIM, (h + 1) * V_DIM)
        o = _diff_attn_head(q_ref[:, sl], [(k_ref, v_ref, h * V_DIM)], lam, subg, lam_init)
        o_ref[:, sl] = o.astype(o_ref.dtype)


def _lat_attn_kernel(lam_ref, subg_ref, q_ref, kc_ref, vc_ref, k_ref, v_ref, o_ref, kcb_ref, vcb_ref,
                     *, lam_init):
    @pl.when(pl.program_id(2) == 0)
    def _():
        kcb_ref[...] = kc_ref[...].astype(BF16)
        vcb_ref[...] = vc_ref[...].astype(BF16)

    lam = _lambda_value(lam_ref, lam_init)
    segs = [(kcb_ref, vcb_ref, 0), (k_ref, v_ref, 0)]
    o = _diff_attn_head(q_ref[...], segs, lam, subg_ref[...], lam_init)
    o_ref[...] = o.astype(o_ref.dtype)


def _attention(q, k, v, cache_k, cache_v, lam_vecs, subg, lam_init):
    small = [pl.BlockSpec((N_COND, HEAD_DIM), lambda *_: (0, 0)),
             pl.BlockSpec((1, V_DIM), lambda *_: (0, 0))]
    o_ctx = pl.pallas_call(
        functools.partial(_ctx_attn_kernel, lam_init=lam_init),
        out_shape=jax.ShapeDtypeStruct((N_CTX, N_HEADS * V_DIM), BF16),
        grid=(BATCH,),
        in_specs=small + [pl.BlockSpec((SEQ, N_HEADS * V_DIM), lambda b: (b, 0))] * 3,
        out_specs=pl.BlockSpec((SEQ, N_HEADS * V_DIM), lambda b: (b, 0)),
        compiler_params=_cparams(1),
        name="attn_context",
    )(lam_vecs, subg, q, k, v)
    nq = DEC_SEQ // TQ_LAT
    q0 = N_CTX // TQ_LAT
    kv0 = N_CTX // DEC_SEQ
    o_lat = pl.pallas_call(
        functools.partial(_lat_attn_kernel, lam_init=lam_init),
        out_shape=jax.ShapeDtypeStruct((N_LAT, N_HEADS * V_DIM), BF16),
        grid=(DEC_BATCH, N_HEADS, nq),
        in_specs=small + [
            pl.BlockSpec((TQ_LAT, V_DIM), lambda b, h, i: (q0 + b * nq + i, h)),
            pl.BlockSpec((PAST_LEN, V_DIM), lambda b, h, i: (b, h)),
            pl.BlockSpec((PAST_LEN, V_DIM), lambda b, h, i: (b, h)),
            pl.BlockSpec((DEC_SEQ, V_DIM), lambda b, h, i: (kv0 + b, h)),
            pl.BlockSpec((DEC_SEQ, V_DIM), lambda b, h, i: (kv0 + b, h)),
        ],
        out_specs=pl.BlockSpec((TQ_LAT, V_DIM), lambda b, h, i: (b * nq + i, h)),
        scratch_shapes=[pltpu.VMEM((PAST_LEN, V_DIM), BF16), pltpu.VMEM((PAST_LEN, V_DIM), BF16)],
        compiler_params=_cparams(3),
        name="attn_latent",
    )(lam_vecs, subg, q, cache_k, cache_v, k, v)
    return jnp.concatenate([o_ctx, o_lat], axis=0)


def _pool_kernel(x_ref, xp_ref, xn_ref, g_ref, shift_ref, scale_ref, gate_ref, w_ref, ps_ref, o_ref):
    i = pl.program_id(0)
    tm = x_ref.shape[0]
    row0 = i * tm
    seq_pos = jnp.where(row0 < N_CTX, row0 % SEQ, (row0 - N_CTX) % DEC_SEQ)
    seq_len = jnp.where(row0 < N_CTX, SEQ, DEC_SEQ)
    has_prev = seq_pos > 0
    has_next = seq_pos + tm < seq_len
    g, shift, scale = g_ref[...], shift_ref[...], scale_ref[...]
    x = x_ref[...]
    h = _rms_mod(x, g, shift, scale)
    halo = xp_ref.shape[0]
    hp = _rms_mod(xp_ref[...], g, shift, scale) * has_prev.astype(F32)
    hn = _rms_mod(xn_ref[...], g, shift, scale) * has_next.astype(F32)
    t_c = lax.broadcasted_iota(jnp.int32, (tm, tm), 0)
    s_c = lax.broadcasted_iota(jnp.int32, (tm, tm), 1)
    t_h = lax.broadcasted_iota(jnp.int32, (tm, halo), 0)
    s_h = lax.broadcasted_iota(jnp.int32, (tm, halo), 1)
    t_1 = lax.broadcasted_iota(jnp.int32, (tm, 1), 0)
    for grp, win in enumerate(POOL_WINDOWS):
        half = win // 2
        sl = slice(grp * POOL_GROUP, (grp + 1) * POOL_GROUP)
        band_c = ((s_c >= t_c - half) & (s_c < t_c + half)).astype(BF16)
        band_p = (s_h - halo >= t_h - half).astype(BF16)
        band_n = (s_h + tm < t_h + half).astype(BF16)
        tot = None
        for band, src in ((band_c, h[:, sl]), (band_p, hp[:, sl]), (band_n, hn[:, sl])):
            hi, lo = _split_bf16(src)
            part = _dot(band, hi) + _dot(band, lo)
            tot = part if tot is None else tot + part
        lo_edge = jnp.where(has_prev, t_1 - half, jnp.maximum(t_1 - half, 0))
        hi_edge = jnp.where(has_next, t_1 + half, jnp.minimum(t_1 + half, tm))
        diff = tot / (hi_edge - lo_edge).astype(F32) - h[:, sl]
        y = _dot(diff.astype(BF16), w_ref[grp]) * ps_ref[:, sl]
        o_ref[:, sl] = x[:, sl] + gate_ref[:, sl] * y


def _pool_mixer(x, g, mod, layer, pool_w, j, pool_scale, tm=TM_POOL, halo=8):
    t, d = x.shape
    nb = tm // halo
    last = t // halo - 1
    return pl.pallas_call(
        _pool_kernel,
        out_shape=jax.ShapeDtypeStruct((t, d), F32),
        grid=(t // tm,),
        in_specs=[
            pl.BlockSpec((tm, d), lambda i: (i, 0)),
            pl.BlockSpec((halo, d), lambda i: (jnp.maximum(i * nb - 1, 0), 0)),
            pl.BlockSpec((halo, d), lambda i: (jnp.minimum((i + 1) * nb, last), 0)),
            pl.BlockSpec((1, d), lambda i: (0, 0)),
            _mod_spec(layer, 0, tm, d), _mod_spec(layer, 1, tm, d), _mod_spec(layer, 2, tm, d),
            pl.BlockSpec((None, len(POOL_WINDOWS), POOL_GROUP, POOL_GROUP), lambda i: (j, 0, 0, 0)),
            pl.BlockSpec((1, d), lambda i: (0, 0)),
        ],
        out_specs=pl.BlockSpec((tm, d), lambda i: (i, 0)),
        compiler_params=_cparams(1),
        name="pool_mixer",
    )(x, x, x, g, mod, mod, mod, pool_w, pool_scale)


def _gmlp_out_kernel(u_ref, v_ref, lg_ref, lb_ref, ws_ref, bs_ref, w_ref, res_ref, gate_ref, o_ref, a_ref):
    @pl.when(pl.program_id(1) == 0)
    def _():
        v = v_ref[...].astype(F32)
        mu = jnp.mean(v, axis=-1, keepdims=True)
        vc = v - mu
        vn = vc * lax.rsqrt(jnp.mean(vc * vc, axis=-1, keepdims=True) + EPS) * lg_ref[...] + lb_ref[...]
        vn = vn.astype(BF16)
        gw = GMLP_WIDTH // GMLP_GROUPS
        for c in range(v.shape[0] // CHUNK):
            rows = slice(c * CHUNK, (c + 1) * CHUNK)
            for grp in range(GMLP_GROUPS):
                cols = slice(grp * gw, (grp + 1) * gw)
                mixed = _dot(ws_ref[grp], vn[rows, cols]) + bs_ref[:, grp:grp + 1]
                a_ref[rows, cols] = (u_ref[rows, cols].astype(F32) * mixed).astype(BF16)

    acc = _dot(a_ref[...], w_ref[...])
    o_ref[...] = res_ref[...] + gate_ref[...] * acc


def _gmlp_out(z, ln_g, ln_b, w_s, b_s_t, w_out, j, res, mod, layer, gate_idx, tm=TM_GMLP, tn=TN_LIN):
    t = z.shape[0]
    d = w_out.shape[-1]
    return pl.pallas_call(
        _gmlp_out_kernel,
        out_shape=jax.ShapeDtypeStruct((t, d), F32),
        grid=(t // tm, d // tn),
        in_specs=[
            pl.BlockSpec((tm, GMLP_WIDTH), lambda i, n: (i, 0)),
            pl.BlockSpec((tm, GMLP_WIDTH), lambda i, n: (i, 1)),
            pl.BlockSpec((1, GMLP_WIDTH), lambda i, n: (0, 0)),
            pl.BlockSpec((1, GMLP_WIDTH), lambda i, n: (0, 0)),
            pl.BlockSpec((None, GMLP_GROUPS, CHUNK, CHUNK), lambda i, n: (j, 0, 0, 0)),
            pl.BlockSpec((CHUNK, GMLP_GROUPS), lambda i, n: (0, 0)),
            pl.BlockSpec((None, GMLP_WIDTH, tn), lambda i, n: (j, 0, n)),
            pl.BlockSpec((tm, tn), lambda i, n: (i, n)),
            _mod_spec(layer, gate_idx, tm, tn, col_axis=1),
        ],
        out_specs=pl.BlockSpec((tm, tn), lambda i, n: (i, n)),
        scratch_shapes=[pltpu.VMEM((tm, GMLP_WIDTH), BF16)],
        compiler_params=_cparams(2),
        name="gmlp_out",
    )(z, z, ln_g, ln_b, w_s, b_s_t, w_out, res, mod)


def _router_kernel(x_ref, g_ref, shift_ref, scale_ref, w_ref, b_ref, idx_ref, gates_ref, rank_ref, cnt_ref,
                   run_ref):
    tm = x_ref.shape[0]

    @pl.when(pl.program_id(0) == 0)
    def _():
        run_ref[...] = jnp.zeros(run_ref.shape, F32)

    h = _rms_mod(x_ref[...], g_ref[...], shift_ref[...], scale_ref[...])
    logits = _dot3(h, w_ref[...]) + b_ref[...]
    lane = lax.broadcasted_iota(jnp.int32, logits.shape, 1).astype(F32)
    vals, idxs = [], []
    for _ in range(TOP_K):
        m = jnp.max(logits, axis=-1, keepdims=True)
        am = jnp.min(jnp.where(logits == m, lane, float(N_EXPERTS)), axis=-1, keepdims=True)
        vals.append(m)
        idxs.append(am)
        logits = jnp.where(lane == am, -jnp.inf, logits)
    exps = [jnp.exp(v - vals[0]) for v in vals]
    den = functools.reduce(jnp.add, exps)
    wide = lax.broadcasted_iota(jnp.int32, (tm, LANES), 1).astype(F32)
    chosen = functools.reduce(jnp.add, [(wide == e).astype(F32) for e in idxs])
    row = lax.broadcasted_iota(jnp.int32, (tm, tm), 0)
    col = lax.broadcasted_iota(jnp.int32, (tm, tm), 1)
    before = _dot((col < row).astype(BF16), chosen.astype(BF16)) + run_ref[...]
    run_ref[...] = run_ref[...] + jnp.sum(chosen, axis=0, keepdims=True)
    out_lane = lax.broadcasted_iota(jnp.int32, idx_ref.shape, 1)
    idx_out = jnp.zeros(idx_ref.shape, F32)
    gate_out = jnp.zeros(gates_ref.shape, F32)
    rank_out = jnp.zeros(rank_ref.shape, F32)
    for r in range(TOP_K):
        idx_out = jnp.where(out_lane == r, idxs[r], idx_out)
        gate_out = jnp.where(out_lane == r, exps[r] / den, gate_out)
        rank_r = jnp.sum(jnp.where(wide == idxs[r], before, 0.0), axis=-1, keepdims=True)
        rank_out = jnp.where(out_lane == r, rank_r, rank_out)
    idx_ref[...] = idx_out.astype(jnp.int32)
    gates_ref[...] = gate_out
    rank_ref[...] = rank_out.astype(jnp.int32)
    cnt_ref[...] = jnp.broadcast_to(run_ref[...], cnt_ref.shape).astype(jnp.int32)


def _router(x, g, mod, layer, w_router, b_router, tm=TM_TOK):
    t, d = x.shape
    tok_out = lambda: pl.BlockSpec((tm, LANES), lambda i: (i, 0))
    return pl.pallas_call(
        _router_kernel,
        out_shape=[jax.ShapeDtypeStruct((t, LANES), jnp.int32),
                   jax.ShapeDtypeStruct((t, LANES), F32),
                   jax.ShapeDtypeStruct((t, LANES), jnp.int32),
                   jax.ShapeDtypeStruct((N_COND, LANES), jnp.int32)],
        grid=(t // tm,),
        in_specs=[
            pl.BlockSpec((tm, d), lambda i: (i, 0)),
            pl.BlockSpec((1, d), lambda i: (0, 0)),
            _mod_spec(layer, 3, tm, d), _mod_spec(layer, 4, tm, d),
            pl.BlockSpec((None, d, N_EXPERTS), lambda i: (layer, 0, 0)),
            pl.BlockSpec((None, 1, N_EXPERTS), lambda i: (layer, 0, 0)),
        ],
        out_specs=[tok_out(), tok_out(), tok_out(), pl.BlockSpec((N_COND, LANES), lambda i: (0, 0))],
        scratch_shapes=[pltpu.VMEM((1, LANES), F32)],
        compiler_params=_cparams(1),
        name="moe_router",
    )(x, g, mod, mod, w_router, b_router)


def _dispatch_kernel(dest_ref, pad_ref, x_ref, g_ref, shift_ref, scale_ref, xs_ref, h_ref, z_ref, sem, pad_sem):
    i = pl.program_id(0)
    tm = x_ref.shape[0]

    @pl.when(i == 0)
    def _():
        z_ref[...] = jnp.zeros(z_ref.shape, F32)

    slot = i % 2
    h = _rms_mod(x_ref[...], g_ref[...], shift_ref[...], scale_ref[...])
    h_ref[slot] = h.reshape(h_ref.shape[1:])
    base = i * (tm * TOP_K)

    def issue(r, carry):
        for kk in range(TOP_K):
            pltpu.make_async_copy(h_ref.at[slot, r], xs_ref.at[dest_ref[base + r * TOP_K + kk]],
                                  sem.at[slot]).start()
        return carry

    lax.fori_loop(0, tm, issue, 0)

    def issue_pad(c, carry):
        pltpu.make_async_copy(z_ref.at[c], xs_ref.at[pad_ref[i * PADS_PER_STEP + c]], pad_sem).start()
        return carry

    lax.fori_loop(0, PADS_PER_STEP, issue_pad, 0)

    def drain(s):
        for _ in range(TOP_K):
            pltpu.make_async_copy(h_ref.at[s], xs_ref.at[pl.ds(0, tm)], sem.at[s]).wait()

    @pl.when(i > 0)
    def _():
        drain(1 - slot)

    @pl.when(i == pl.num_programs(0) - 1)
    def _():
        drain(slot)

    pltpu.make_async_copy(z_ref, xs_ref.at[pl.ds(0, PADS_PER_STEP)], pad_sem).wait()


def _dispatch(x, g, mod, layer, dest, pad_rows, tm=TM_TOK):
    t, d = x.shape
    return pl.pallas_call(
        _dispatch_kernel,
        out_shape=jax.ShapeDtypeStruct((N_MOE_ROWS, SLABS, LANES), F32),
        grid_spec=pltpu.PrefetchScalarGridSpec(
            num_scalar_prefetch=2,
            grid=(t // tm,),
            in_specs=[
                pl.BlockSpec((tm, d), lambda i, *_: (i, 0)),
                pl.BlockSpec((1, d), lambda i, *_: (0, 0)),
                _mod_spec(layer, 3, tm, d), _mod_spec(layer, 4, tm, d),
            ],
            out_specs=pl.BlockSpec(memory_space=pl.ANY),
            scratch_shapes=[pltpu.VMEM((2, tm, SLABS, LANES), F32), pltpu.VMEM((PADS_PER_STEP, SLABS, LANES), F32),
                            pltpu.SemaphoreType.DMA((2,)), pltpu.SemaphoreType.DMA],
        ),
        compiler_params=_cparams(1),
        name="moe_dispatch",
    )(dest, pad_rows, x, g, mod, mod)


def _expert_changed(te_ref, i):
    return (i == 0) | (te_ref[i] != te_ref[jnp.maximum(i - 1, 0)])


def _expert_up_kernel(te_ref, nu_ref, x_ref, wg_ref, wu_ref, bg_ref, bu_ref, o_ref, wgb_ref, wub_ref):
    i = pl.program_id(1)

    @pl.when(_expert_changed(te_ref, i))
    def _():
        wgb_ref[...] = wg_ref[...].astype(BF16)
        wub_ref[...] = wu_ref[...].astype(BF16)

    @pl.when(i < nu_ref[0])
    def _():
        x = x_ref[...].reshape(x_ref.shape[0], D_MODEL).astype(BF16)
        gate = jnp.minimum(_dot(x, wgb_ref[...]) + bg_ref[...], SWIGLU_LIMIT)
        up = jnp.clip(_dot(x, wub_ref[...]) + bu_ref[...], -SWIGLU_LIMIT, SWIGLU_LIMIT)
        act = gate * (1.0 / (1.0 + jnp.exp(-SWIGLU_ALPHA * gate))) * (up + 1.0)
        o_ref[...] = act.astype(o_ref.dtype)

    @pl.when(i >= nu_ref[0])
    def _():
        o_ref[...] = jnp.zeros(o_ref.shape, o_ref.dtype)


def _expert_down_kernel(te_ref, nu_ref, a_ref, w_ref, b_ref, o_ref, wb_ref):
    i = pl.program_id(0)

    @pl.when(_expert_changed(te_ref, i))
    def _():
        wb_ref[...] = w_ref[...].astype(BF16)

    @pl.when(i < nu_ref[0])
    def _():
        y = _dot(a_ref[...], wb_ref[...]) + b_ref[...]
        o_ref[...] = y.reshape(o_ref.shape)

    @pl.when(i >= nu_ref[0])
    def _():
        o_ref[...] = jnp.zeros(o_ref.shape, o_ref.dtype)


def _experts(xs, tile_expert, n_used, layer, w_gu, b_gu, w_down, b_down, tm=TM_MOE, th=TH_MOE):
    d = D_MODEL
    n_j = D_EXPERT // th
    row = lambda i, nu: jnp.minimum(i, jnp.maximum(nu[0] - 1, 0))
    act = pl.pallas_call(
        _expert_up_kernel,
        out_shape=jax.ShapeDtypeStruct((N_MOE_ROWS, D_EXPERT), BF16),
        grid_spec=pltpu.PrefetchScalarGridSpec(
            num_scalar_prefetch=2,
            grid=(n_j, N_MOE_TILES),
            in_specs=[
                pl.BlockSpec((tm, SLABS, LANES), lambda j, i, te, nu: (row(i, nu), 0, 0)),
                pl.BlockSpec((None, None, d, th), lambda j, i, te, nu: (layer, te[i], 0, j)),
                pl.BlockSpec((None, None, d, th), lambda j, i, te, nu: (layer, te[i], 0, n_j + j)),
                pl.BlockSpec((None, None, 1, th), lambda j, i, te, nu: (layer, te[i], 0, j)),
                pl.BlockSpec((None, None, 1, th), lambda j, i, te, nu: (layer, te[i], 0, n_j + j)),
            ],
            out_specs=pl.BlockSpec((tm, th), lambda j, i, te, nu: (i, j)),
            scratch_shapes=[pltpu.VMEM((d, th), BF16), pltpu.VMEM((d, th), BF16)],
        ),
        compiler_params=_cparams(2),
        name="moe_expert_up",
    )(tile_expert, n_used, xs, w_gu, w_gu, b_gu, b_gu)
    return pl.pallas_call(
        _expert_down_kernel,
        out_shape=jax.ShapeDtypeStruct((N_MOE_ROWS, SLABS, LANES), F32),
        grid_spec=pltpu.PrefetchScalarGridSpec(
            num_scalar_prefetch=2,
            grid=(N_MOE_TILES,),
            in_specs=[
                pl.BlockSpec((tm, D_EXPERT), lambda i, te, nu: (row(i, nu), 0)),
                pl.BlockSpec((None, None, D_EXPERT, d), lambda i, te, nu: (layer, te[i], 0, 0)),
                pl.BlockSpec((None, None, 1, d), lambda i, te, nu: (layer, te[i], 0, 0)),
            ],
            out_specs=pl.BlockSpec((tm, SLABS, LANES), lambda i, te, nu: (i, 0, 0)),
            scratch_shapes=[pltpu.VMEM((D_EXPERT, d), BF16)],
        ),
        compiler_params=_cparams(1),
        name="moe_expert_down",
    )(tile_expert, n_used, act, w_down, b_down)


def _combine_kernel(dest_ref, x_ref, gates_ref, gate_ref, y_ref, o_ref, buf_ref, sem):
    i = pl.program_id(0)
    n = pl.num_programs(0)
    tm = x_ref.shape[0]

    def issue(tile, slot):
        base = tile * (tm * TOP_K)

        def body(r, carry):
            for kk in range(TOP_K):
                pltpu.make_async_copy(y_ref.at[dest_ref[base + r * TOP_K + kk]],
                                      buf_ref.at[slot, kk * tm + r], sem.at[slot]).start()
            return carry

        lax.fori_loop(0, tm, body, 0)

    @pl.when(i == 0)
    def _():
        issue(0, 0)

    @pl.when(i + 1 < n)
    def _():
        issue(i + 1, (i + 1) % 2)

    slot = i % 2
    pltpu.make_async_copy(y_ref.at[pl.ds(0, TOP_K * tm)], buf_ref.at[slot], sem.at[slot]).wait()
    gates = gates_ref[...]
    acc = None
    for kk in range(TOP_K):
        part = gates[:, kk:kk + 1] * buf_ref[slot, pl.ds(kk * tm, tm)].reshape(tm, D_MODEL)
        acc = part if acc is None else acc + part
    o_ref[...] = x_ref[...] + gate_ref[...] * acc


def _combine(x, y, gates, dest, mod, layer, tm=TM_TOK):
    t, d = x.shape
    return pl.pallas_call(
        _combine_kernel,
        out_shape=jax.ShapeDtypeStruct((t, d), F32),
        grid_spec=pltpu.PrefetchScalarGridSpec(
            num_scalar_prefetch=1,
            grid=(t // tm,),
            in_specs=[
                pl.BlockSpec((tm, d), lambda i, *_: (i, 0)),
                pl.BlockSpec((tm, LANES), lambda i, *_: (i, 0)),
                _mod_spec(layer, 5, tm, d),
                pl.BlockSpec(memory_space=pl.ANY),
            ],
            out_specs=pl.BlockSpec((tm, d), lambda i, *_: (i, 0)),
            scratch_shapes=[pltpu.VMEM((2, TOP_K * tm, SLABS, LANES), F32), pltpu.SemaphoreType.DMA((2,))],
        ),
        compiler_params=_cparams(1),
        name="moe_combine",
    )(dest, x, gates, mod, y)


def _dispatch_plan(idx, rank, counts, tm=TM_MOE):
    padded = (counts + tm - 1) // tm * tm
    pend = jnp.cumsum(padded)
    pstart = pend - padded
    experts = jnp.arange(N_EXPERTS, dtype=jnp.int32)
    flat_e = idx.reshape(-1)
    dest = rank.reshape(-1) + jnp.sum(jnp.where(flat_e[:, None] == experts[None, :], pstart[None, :], 0), axis=1)
    tile_start = jnp.arange(N_MOE_TILES, dtype=jnp.int32) * tm
    tile_expert = jnp.minimum(jnp.sum((pend[None, :] <= tile_start[:, None]).astype(jnp.int32), axis=1),
                              N_EXPERTS - 1)
    n_used = (pend[-1] // tm).astype(jnp.int32).reshape(1)
    seg_len = jnp.concatenate([padded - counts, (N_MOE_ROWS - pend[-1])[None]])
    seg_start = jnp.concatenate([pstart + counts, pend[-1:]])
    seg_end = jnp.cumsum(seg_len)
    seg_lo = seg_end - seg_len
    slot = jnp.arange(N_PAD_SLOTS, dtype=jnp.int32)[:, None]
    in_seg = (slot >= seg_lo[None, :]) & (slot < seg_end[None, :])
    pad_rows = slot[:, 0] + jnp.sum(jnp.where(in_seg, (seg_start - seg_lo)[None, :], 0), axis=1)
    return dest.astype(jnp.int32), pad_rows.astype(jnp.int32), tile_expert.astype(jnp.int32), n_used


def _moe_block(x, g, mod, layer, w_router, b_router, w_gu, b_gu, w_down, b_down):
    idx, gates, rank, cnt = _router(x, g, mod, layer, w_router, b_router)
    dest, pad_rows, tile_expert, n_used = _dispatch_plan(idx[:, :TOP_K], rank[:, :TOP_K], cnt[0, :N_EXPERTS])
    xs = _dispatch(x, g, mod, layer, dest, pad_rows)
    y = _experts(xs, tile_expert, n_used, layer, w_gu, b_gu, w_down, b_down)
    return _combine(x, y, gates, dest, mod, layer)


def _rope_tables():
    t = jnp.arange(DEC_SEQ)
    row = (t // GRID_W).astype(F32)
    col = (t % GRID_W).astype(F32)
    n_freq = HEAD_DIM // 4
    inv = ROPE_THETA ** (-jnp.arange(n_freq, dtype=F32) / n_freq)
    ang = jnp.concatenate([row[:, None] * inv] * 2 + [col[:, None] * inv] * 2, axis=1)
    cos = jnp.concatenate([jnp.ones((N_CTX, HEAD_DIM), F32)] + [jnp.cos(ang)] * DEC_BATCH, axis=0)
    sin = jnp.concatenate([jnp.zeros((N_CTX, HEAD_DIM), F32)] + [jnp.sin(ang)] * DEC_BATCH, axis=0)
    return cos, sin


def kernel(x_prompt, x_sample, c, cache_k, cache_v, c_ctx, w_mod, b_mod, norm1_g, norm2_g, attn_w_qkv, attn_q_norm, attn_k_norm, attn_lambda_q1, attn_lambda_k1, attn_lambda_q2, attn_lambda_k2, attn_subln_g, attn_w_o, pool_w, pool_scale, gmlp_w_in, gmlp_ln_g, gmlp_ln_b, gmlp_w_s, gmlp_b_s, gmlp_w_out, moe_w_router, moe_b_router, moe_w_gu, moe_b_gu, moe_w_down, moe_b_down):
    d = D_MODEL
    x = jnp.concatenate([x_prompt.reshape(N_CTX, d), x_sample.reshape(N_LAT, d)], axis=0)
    cond = jnp.concatenate([c_ctx[None, :], c, jnp.zeros((N_COND - 1 - DEC_BATCH, d), F32)], axis=0)
    mod = _modulation(cond, w_mod, b_mod).reshape(DEPTH, N_COND, 6, 1, d)
    cos, sin = _rope_tables()
    w_qkv, w_o, w_pool = attn_w_qkv.astype(BF16), attn_w_o.astype(BF16), pool_w.astype(BF16)
    w_in, w_s, w_out = gmlp_w_in.astype(BF16), gmlp_w_s.astype(BF16), gmlp_w_out.astype(BF16)
    b_router = moe_b_router.reshape(DEPTH, 1, N_EXPERTS)
    b_gu = moe_b_gu.reshape(DEPTH, N_EXPERTS, 1, 2 * D_EXPERT)
    b_down = moe_b_down.reshape(DEPTH, N_EXPERTS, 1, d)
    ctx_k, ctx_v = [], []
    for layer in range(DEPTH):
        kind, j = layer % 3, layer // 3
        g1 = norm1_g[layer].reshape(1, d)
        norm1 = (g1, mod, layer, 0, 1)
        if kind == 0:
            lam_init = 0.8 - 0.6 * math.exp(-0.3 * layer)
            (q,) = _linear(x, w_qkv, j, 0, QK_COLS, [BF16], norm=norm1, epi="qk",
                           qk=(attn_q_norm[j].reshape(1, HEAD_DIM), cos, sin),
                           out_scale=HEAD_DIM ** -0.5 * math.log2(math.e))
            k32, k16 = _linear(x, w_qkv, j, QK_COLS, QK_COLS, [F32, BF16], norm=norm1, epi="qk",
                               qk=(attn_k_norm[j].reshape(1, HEAD_DIM), cos, sin))
            v32, v16 = _linear(x, w_qkv, j, 2 * QK_COLS, N_HEADS * V_DIM, [F32, BF16], norm=norm1)
            ctx_k.append(k32[:N_CTX].reshape(BATCH, SEQ, N_HEADS, 2, HEAD_DIM))
            ctx_v.append(v32[:N_CTX].reshape(BATCH, SEQ, N_HEADS, V_DIM))
            lam_vecs = jnp.concatenate(
                [attn_lambda_q1[j][None], attn_lambda_k1[j][None], attn_lambda_q2[j][None],
                 attn_lambda_k2[j][None], jnp.zeros((N_COND - 4, HEAD_DIM), F32)], axis=0)
            o = _attention(q, k16, v16,
                           cache_k[:, j].reshape(DEC_BATCH * PAST_LEN, QK_COLS),
                           cache_v[:, j].reshape(DEC_BATCH * PAST_LEN, N_HEADS * V_DIM),
                           lam_vecs, attn_subln_g[j].reshape(1, V_DIM), lam_init)
            (x,) = _linear(o, w_o, j, 0, d, [F32], epi="residual", residual=(x, mod, layer, 2))
        elif kind == 1:
            x = _pool_mixer(x, g1, mod, layer, w_pool, j, pool_scale[j].reshape(1, d))
        else:
            (z,) = _linear(x, w_in, j, 0, 2 * GMLP_WIDTH, [BF16], norm=norm1, epi="gelu")
            x = _gmlp_out(z, gmlp_ln_g[j].reshape(1, -1), gmlp_ln_b[j].reshape(1, -1), w_s,
                          gmlp_b_s[j].T, w_out, j, x, mod, layer, 2)
        x = _moe_block(x, norm2_g[layer].reshape(1, d), mod, layer, moe_w_router, b_router,
                       moe_w_gu, b_gu, moe_w_down, b_down)
    y_prompt = x[:N_CTX].reshape(BATCH, SEQ, d)
    y_sample = x[N_CTX:].reshape(DEC_BATCH, DEC_SEQ, d)
    return (y_prompt, y_sample, jnp.stack(ctx_k, axis=1), jnp.stack(ctx_v, axis=1))
```
